```python
import jax
import jax.numpy as jnp
from jax import lax
import numpy as np

D_MODEL = 2048
BATCH = 8
SEQ = 2048
DEPTH = 2
DEC_BATCH = 128
DEC_SEQ = 8
PAST_LEN = 2048
PAGE_SIZE = 128

N_HEADS = 16
HEAD_DIM = D_MODEL // N_HEADS
N_KV_HEADS = 4
GROUP = N_HEADS // N_KV_HEADS
CMP_BLOCK = 32
CMP_STRIDE = 16
CMP_HIDDEN = HEAD_DIM
SEL_BLOCK = 64
N_SEL = 16
WINDOW = 512
Q_BLOCK = 128
N_BRANCH = 3
SCALE = HEAD_DIM ** -0.5
NSA_IN_COLS = N_HEADS * HEAD_DIM + N_BRANCH * 2 * N_KV_HEADS * HEAD_DIM + N_HEADS * N_BRANCH
POOL_WINDOWS = (2, 4, 8, 16)
POOL_GROUP = D_MODEL // len(POOL_WINDOWS)
POOL_BUF = max(POOL_WINDOWS) - 1
D_FF = 11 * D_MODEL // 4
N_EXPERTS = 8
TOP_K = 2
D_FF_EXPERT = 7 * D_MODEL // 2
MOE_BLOCK = 256
ALPHA = (2.0 * DEPTH) ** 0.25
BETA = (8.0 * DEPTH) ** -0.25
LN_EPS = 1e-5
BIG = 1e9
N_A = (DEPTH + 1) // 2
N_B = DEPTH // 2

kernel_name = 'nsa_pool_hybrid_step'


def _masked_softmax(s, mask, axis=-1):
    s = jnp.where(mask, s.astype(jnp.float32), -jnp.inf)
    m = jnp.max(s, axis=axis, keepdims=True)
    m = jnp.where(jnp.isfinite(m), m, 0.0)
    e = jnp.exp(s - m)
    return e / jnp.maximum(jnp.sum(e, axis=axis, keepdims=True), 1e-30)


def _adaln(c, w, b):
    m = jnp.dot(jax.nn.silu(c), w) + b
    return jnp.split(m[:, None, :], 6, axis=-1)


def _modulate(x, shift, scale):
    return x * (1.0 + scale) + shift


def _post_norm(x, y, g, b):
    z = ALPHA * x.astype(jnp.float32) + y.astype(jnp.float32)
    mu = jnp.mean(z, axis=-1, keepdims=True)
    var = jnp.mean(jnp.square(z - mu), axis=-1, keepdims=True)
    return ((z - mu) * lax.rsqrt(var + LN_EPS) * g + b).astype(x.dtype)


def _swiglu(x, w_up, w_down):
    g, u = jnp.split(x @ w_up, 2, axis=-1)
    return (jax.nn.silu(g) * u) @ w_down


def _nsa_project(h, w_in):
    B, T, _ = h.shape
    p = jnp.einsum('btd,de->bte', h, w_in)
    nq = N_HEADS * HEAD_DIM
    nkv = N_BRANCH * 2 * N_KV_HEADS * HEAD_DIM
    q = p[..., :nq].reshape(B, T, N_KV_HEADS, GROUP, HEAD_DIM)
    kv = p[..., nq:nq + nkv].reshape(B, T, N_BRANCH, 2, N_KV_HEADS, HEAD_DIM)
    gates = jax.nn.sigmoid(p[..., nq + nkv:].astype(jnp.float32)).reshape(B, T, N_KV_HEADS, GROUP, N_BRANCH)
    return q, kv[:, :, 0], kv[:, :, 1], kv[:, :, 2], gates


def _compress(kv, w_c1, w_c2, c_pos):
    B, L = kv.shape[:2]
    R = CMP_BLOCK // CMP_STRIDE
    n_chunk = L // CMP_STRIDE
    nc = n_chunk - R + 1
    chunks = kv[:, :n_chunk * CMP_STRIDE].reshape(B, n_chunk, CMP_STRIDE, 2, N_KV_HEADS, HEAD_DIM)
    w1 = w_c1.reshape(2, R, CMP_STRIDE, HEAD_DIM, CMP_HIDDEN)
    part = jnp.einsum('bnikhc,krico->bnrkho', chunks, w1)
    hsum = sum(part[:, r:r + nc, r] for r in range(R))
    pos_bias = jnp.einsum('klc,klco->ko', c_pos, w_c1.reshape(2, CMP_BLOCK, HEAD_DIM, CMP_HIDDEN))
    hid = jax.nn.gelu(hsum + pos_bias[:, None, :])
    return jnp.einsum('bnkho,koc->bnkhc', hid, w_c2)


def _overlap(nc, ns):
    c0 = jnp.arange(nc)[:, None] * CMP_STRIDE
    s0 = jnp.arange(ns)[None, :] * SEL_BLOCK
    ov = jnp.minimum(c0 + CMP_BLOCK, s0 + SEL_BLOCK) - jnp.maximum(c0, s0)
    return jnp.maximum(ov, 0).astype(jnp.float32) / CMP_BLOCK


def _cmp_branch(q, kvc, qpos, ns):
    nc = kvc.shape[1]
    s = jnp.einsum('btgjd,bngd->btgjn', q, kvc[:, :, 0]) * SCALE
    visible = (jnp.arange(nc) * CMP_STRIDE + CMP_BLOCK - 1)[None, :] <= qpos[:, None]
    p = _masked_softmax(s, visible[None, :, None, None, :])
    o = jnp.einsum('btgjn,bngd->btgjd', p.astype(q.dtype), kvc[:, :, 1])
    p_slc = jnp.einsum('btgjn,nm->btgm', p, _overlap(nc, ns))
    return o, p_slc


def _select_blocks(p_slc, qpos, ns):
    blk = jnp.arange(ns)[None, :]
    cur = (qpos // SEL_BLOCK)[:, None]
    forced = (blk == 0) | (blk == cur) | (blk == cur - 1)
    score = jnp.where(forced[None, :, None, :], BIG, p_slc)
    score = jnp.where((blk > cur)[None, :, None, :], -BIG, score)
    val, idx = lax.top_k(score, min(N_SEL, ns))
    return idx, val > -0.5 * BIG


def _sel_core(q, blocks, idx, ok, qpos):
    kpos = idx[..., None] * SEL_BLOCK + jnp.arange(SEL_BLOCK)
    mask = ok[..., None] & (kpos <= qpos[:, None, None, None])
    s = jnp.einsum('tgjd,tgksd->tgjks', q, blocks[..., 0, :]) * SCALE
    p = _masked_softmax(s, mask[:, :, None], axis=(-2, -1))
    return jnp.einsum('tgjks,tgksd->tgjd', p.astype(q.dtype), blocks[..., 1, :])


def _sel_branch_prompt(q, kv_sel, idx, ok):
    B, S = q.shape[:2]
    nqb = S // Q_BLOCK
    ns = S // SEL_BLOCK
    kk = idx.shape[-1]
    g_ar = jnp.arange(N_KV_HEADS)[None, :, None]
    kv_blocks = kv_sel.reshape(B, ns, SEL_BLOCK, 2, N_KV_HEADS, HEAD_DIM)
    qpos = jnp.arange(S).reshape(nqb, Q_BLOCK)

    def per_seq(args):
        q_s, idx_s, ok_s, kv_s = args

        def per_block(a):
            q_b, idx_b, ok_b, pos_b = a
            blocks = kv_s[idx_b, :, :, g_ar]
            return _sel_core(q_b, blocks, idx_b, ok_b, pos_b)

        return lax.map(per_block, (q_s, idx_s, ok_s, qpos))

    o = lax.map(per_seq, (q.reshape(B, nqb, Q_BLOCK, N_KV_HEADS, GROUP, HEAD_DIM),
                          idx.reshape(B, nqb, Q_BLOCK, N_KV_HEADS, kk),
                          ok.reshape(B, nqb, Q_BLOCK, N_KV_HEADS, kk), kv_blocks))
    return o.reshape(B, S, N_KV_HEADS, GROUP, HEAD_DIM)


def _sel_branch_sample(q, kv_new, idx, ok, pool, page_table, qpos):
    DB, T = q.shape[:2]
    n_pages = PAST_LEN // PAGE_SIZE
    bpp = PAGE_SIZE // SEL_BLOCK
    n_past_blk = PAST_LEN // SEL_BLOCK
    ntb = -(-T // SEL_BLOCK)
    tail = jnp.pad(kv_new, ((0, 0), (0, ntb * SEL_BLOCK - T), (0, 0), (0, 0), (0, 0)))
    tail = tail.reshape(DB, ntb, SEL_BLOCK, 2, N_KV_HEADS, HEAD_DIM)
    pool_blocks = pool.reshape(pool.shape[0], bpp, SEL_BLOCK, 2, N_KV_HEADS, HEAD_DIM)
    g_ar = jnp.arange(N_KV_HEADS)[None, :, None]

    def per_seq(a):
        q_s, idx_s, ok_s, pt_s, tail_s = a
        page = pt_s[jnp.clip(idx_s // bpp, 0, n_pages - 1)]
        from_pool = pool_blocks[page, idx_s % bpp, :, :, g_ar]
        from_tail = tail_s[jnp.clip(idx_s - n_past_blk, 0, ntb - 1), :, :, g_ar]
        in_tail = (idx_s >= n_past_blk)[..., None, None, None]
        return _sel_core(q_s, jnp.where(in_tail, from_tail, from_pool), idx_s, ok_s, qpos)

    return lax.map(per_seq, (q, idx, ok, page_table, tail))


def _win_branch_prompt(q, kv_win):
    B, S = q.shape[:2]
    nqb = S // Q_BLOCK
    span = Q_BLOCK + WINDOW
    kvp = jnp.pad(kv_win, ((0, 0), (WINDOW, 0), (0, 0), (0, 0), (0, 0)))
    kidx = jnp.arange(nqb)[:, None] * Q_BLOCK + jnp.arange(span)[None, :]
    kvb = kvp[:, kidx]
    kpos = kidx - WINDOW
    qpos = jnp.arange(S).reshape(nqb, Q_BLOCK)
    dist = qpos[:, :, None] - kpos[:, None, :]
    mask = (kpos[:, None, :] >= 0) & (dist >= 0) & (dist <= WINDOW)
    qb = q.reshape(B, nqb, Q_BLOCK, N_KV_HEADS, GROUP, HEAD_DIM)
    s = jnp.einsum('bqtgjd,bqsgd->bqtgjs', qb, kvb[:, :, :, 0]) * SCALE
    p = _masked_softmax(s, mask[None, :, :, None, None, :])
    o = jnp.einsum('bqtgjs,bqsgd->bqtgjd', p.astype(q.dtype), kvb[:, :, :, 1])
    return o.reshape(B, S, N_KV_HEADS, GROUP, HEAD_DIM)


def _win_branch_sample(q, kv_new, win_buf, qpos):
    wb = win_buf.shape[1]
    kv = jnp.concatenate([win_buf, kv_new], axis=1)
    kpos = PAST_LEN - wb + jnp.arange(kv.shape[1])
    dist = qpos[:, None] - kpos[None, :]
    mask = (dist >= 0) & (dist <= WINDOW)
    s = jnp.einsum('btgjd,bsgd->btgjs', q, kv[:, :, 0]) * SCALE
    p = _masked_softmax(s, mask[None, :, None, None, :])
    o = jnp.einsum('btgjs,bsgd->btgjd', p.astype(q.dtype), kv[:, :, 1])
    return o, kv


def _nsa_combine(o_c, o_s, o_w, gates, w_out):
    o = gates[..., 0:1] * o_c + gates[..., 1:2] * o_s + gates[..., 2:3] * o_w
    B, T = o.shape[:2]
    return jnp.einsum('bte,ed->btd', o.reshape(B, T, -1).astype(w_out.dtype), w_out)


def _nsa_prompt(h, w_in, w_c1, w_c2, c_pos, w_out):
    S = h.shape[1]
    q, kv_c, kv_s, kv_w, gates = _nsa_project(h, w_in)
    qpos = jnp.arange(S)
    ns = -(-S // SEL_BLOCK)
    o_c, p_slc = _cmp_branch(q, _compress(kv_c, w_c1, w_c2, c_pos), qpos, ns)
    idx, ok = _select_blocks(p_slc, qpos, ns)
    o_s = _sel_branch_prompt(q, kv_s, idx, ok)
    o_w = _win_branch_prompt(q, kv_w)
    y = _nsa_combine(o_c, o_s, o_w, gates, w_out)
    return y, kv_c, kv_s, kv_w[:, S - min(WINDOW, S):]


def _nsa_sample(h, cache_cmp, cache_sel, win_buf, page_table, w_in, w_c1, w_c2, c_pos, w_out):
    DB, T, _ = h.shape
    q, kv_c, kv_s, kv_w, gates = _nsa_project(h, w_in)
    qpos = PAST_LEN + jnp.arange(T)
    L = PAST_LEN + T
    ns = -(-L // SEL_BLOCK)
    past_c = cache_cmp[page_table].reshape(DB, PAST_LEN, 2, N_KV_HEADS, HEAD_DIM)
    kvc = _compress(jnp.concatenate([past_c, kv_c], axis=1), w_c1, w_c2, c_pos)
    o_c, p_slc = _cmp_branch(q, kvc, qpos, ns)
    idx, ok = _select_blocks(p_slc, qpos, ns)
    o_s = _sel_branch_sample(q, kv_s, idx, ok, cache_sel, page_table, qpos)
    o_w, kv_win_all = _win_branch_sample(q, kv_w, win_buf, qpos)
    y = _nsa_combine(o_c, o_s, o_w, gates, w_out)
    keep = min(WINDOW, L)
    return y, kv_c, kv_s, kv_win_all[:, kv_win_all.shape[1] - keep:]


def _pool_mix(h_ext, n_new, w_pool, pool_scale):
    P = h_ext.shape[1] - n_new
    cs = jnp.pad(jnp.cumsum(h_ext.astype(jnp.float32), axis=1), ((0, 0), (1, 0), (0, 0)))
    e1 = P + 1 + jnp.arange(n_new)
    x_new = h_ext[:, P:].astype(jnp.float32)
    outs = []
    for gi, w in enumerate(POOL_WINDOWS):
        sl = slice(gi * POOL_GROUP, (gi + 1) * POOL_GROUP)
        lo = jnp.maximum(e1 - w, 0)
        mean = (cs[:, e1, sl] - cs[:, lo, sl]) / (e1 - lo).astype(jnp.float32)[None, :, None]
        outs.append(jnp.einsum('btc,cd->btd', (mean - x_new[..., sl]).astype(h_ext.dtype), w_pool[gi]))
    return jnp.concatenate(outs, axis=-1) * pool_scale


def _moe_ffn(h, w_router, b_router, w_up, w_down):
    B, T, D = h.shape
    n = B * T
    x = h.reshape(n, D)
    logits = (x @ w_router).astype(jnp.float32) + b_router
    top_logit, top_e = lax.top_k(logits, TOP_K)
    gate = jax.nn.softmax(top_logit, axis=-1)
    e_flat = top_e.reshape(-1)
    tok_flat = jnp.repeat(jnp.arange(n, dtype=jnp.int32), TOP_K)
    g_flat = gate.reshape(-1)
    order = jnp.argsort(e_flat, stable=True)
    e_sorted = e_flat[order]
    counts = jnp.bincount(e_flat, length=N_EXPERTS)
    padded = (counts + MOE_BLOCK - 1) // MOE_BLOCK * MOE_BLOCK
    start = jnp.cumsum(counts) - counts
    pstart = jnp.cumsum(padded) - padded
    dest = pstart[e_sorted] + jnp.arange(n * TOP_K) - start[e_sorted]
    n_blocks = -(-(n * TOP_K) // MOE_BLOCK) + N_EXPERTS
    n_slots = n_blocks * MOE_BLOCK
    slot_tok = jnp.full((n_slots,), n, jnp.int32).at[dest].set(tok_flat[order])
    slot_gate = jnp.zeros((n_slots,), jnp.float32).at[dest].set(g_flat[order])
    block_expert = jnp.minimum(
        jnp.searchsorted(jnp.cumsum(padded), jnp.arange(n_blocks) * MOE_BLOCK, side='right'), N_EXPERTS - 1)
    x_pad = jnp.concatenate([x, jnp.zeros((1, D), x.dtype)], axis=0)
    xb = x_pad[slot_tok].reshape(n_blocks, MOE_BLOCK, D)

    def expert_block(a):
        xe, e = a
        return _swiglu(xe, w_up[e], w_down[e])

    yb = lax.map(expert_block, (xb, block_expert)).reshape(n_slots, D)
    y = jnp.zeros((n + 1, D), yb.dtype).at[slot_tok].add(yb * slot_gate[:, None].astype(yb.dtype))
    return y[:n].reshape(B, T, D)


def setup_inputs(seed: int = 0) -> dict:
    key = jax.random.key(seed)
    ks = iter(jax.random.split(key, 32))
    f32 = jnp.float32

    def nrm(shape, scale=1.0):
        return jax.random.normal(next(ks), shape, f32) * scale

    n_pages = PAST_LEN // PAGE_SIZE
    n_used = DEC_BATCH * n_pages
    n_phys = n_used + (n_used + 3) // 4
    wb = min(WINDOW, PAST_LEN)
    x_prompt = nrm((BATCH, SEQ, D_MODEL))
    x_sample = nrm((DEC_BATCH, DEC_SEQ, D_MODEL))
    cache_cmp_kv = nrm((N_A, n_phys, PAGE_SIZE, 2, N_KV_HEADS, HEAD_DIM))
    cache_sel_kv = nrm((N_A, n_phys, PAGE_SIZE, 2, N_KV_HEADS, HEAD_DIM))
    state_win_kv = nrm((N_A, DEC_BATCH, wb, 2, N_KV_HEADS, HEAD_DIM))
    state_pool = nrm((N_B, DEC_BATCH, POOL_BUF, D_MODEL))
    page_table = jax.random.permutation(next(ks), n_phys)[:n_used].reshape(DEC_BATCH, n_pages).astype(jnp.int32)
    c_prompt = nrm((BATCH, D_MODEL))
    c_sample = nrm((DEC_BATCH, D_MODEL))
    w_ada = nrm((DEPTH, D_MODEL, 6 * D_MODEL), 0.5 * D_MODEL ** -0.5)
    b_ada = nrm((DEPTH, 6 * D_MODEL), 0.02)
    ln_gain = 1.0 + nrm((DEPTH, 2, D_MODEL), 0.1)
    ln_bias = nrm((DEPTH, 2, D_MODEL), 0.02)
    w_nsa_in = nrm((N_A, D_MODEL, NSA_IN_COLS), D_MODEL ** -0.5)
    w_cmp1 = nrm((N_A, 2, CMP_BLOCK * HEAD_DIM, CMP_HIDDEN), (CMP_BLOCK * HEAD_DIM) ** -0.5)
    w_cmp2 = nrm((N_A, 2, CMP_HIDDEN, HEAD_DIM), CMP_HIDDEN ** -0.5)
    cmp_pos = nrm((N_A, 2, CMP_BLOCK, HEAD_DIM), 0.1)
    w_nsa_out = nrm((N_A, N_HEADS * HEAD_DIM, D_MODEL), BETA * (N_HEADS * HEAD_DIM) ** -0.5)
    w_pool = nrm((N_B, len(POOL_WINDOWS), POOL_GROUP, POOL_GROUP), BETA * POOL_GROUP ** -0.5)
    pool_scale = 1.0 + nrm((N_B, D_MODEL), 0.1)
    w_ffn_up = nrm((N_A, D_MODEL, 2 * D_FF), D_MODEL ** -0.5)
    w_ffn_down = nrm((N_A, D_FF, D_MODEL), BETA * D_FF ** -0.5)
    w_router = nrm((N_B, D_MODEL, N_EXPERTS), D_MODEL ** -0.5)
    b_router = nrm((N_B, N_EXPERTS), 0.01)
    w_moe_up = nrm((N_B, N_EXPERTS, D_MODEL, 2 * D_FF_EXPERT), D_MODEL ** -0.5)
    w_moe_down = nrm((N_B, N_EXPERTS, D_FF_EXPERT, D_MODEL), BETA * D_FF_EXPERT ** -0.5)
    return {'x_prompt': x_prompt, 'x_sample': x_sample, 'cache_cmp_kv': cache_cmp_kv,
            'cache_sel_kv': cache_sel_kv, 'state_win_kv': state_win_kv, 'state_pool': state_pool,
            'page_table': page_table, 'c_prompt': c_prompt, 'c_sample': c_sample,
            'w_ada': w_ada, 'b_ada': b_ada, 'ln_gain': ln_gain, 'ln_bias': ln_bias,
            'w_nsa_in': w_nsa_in, 'w_cmp1': w_cmp1, 'w_cmp2': w_cmp2, 'cmp_pos': cmp_pos,
            'w_nsa_out': w_nsa_out, 'w_pool': w_pool, 'pool_scale': pool_scale,
            'w_ffn_up': w_ffn_up, 'w_ffn_down': w_ffn_down, 'w_router': w_router, 'b_router': b_router,
            'w_moe_up': w_moe_up, 'w_moe_down': w_moe_down}


def reference(x_prompt, x_sample, cache_cmp_kv, cache_sel_kv, state_win_kv, state_pool, page_table,
              c_prompt, c_sample, w_ada, b_ada, ln_gain, ln_bias, w_nsa_in, w_cmp1, w_cmp2, cmp_pos,
              w_nsa_out, w_pool, pool_scale, w_ffn_up, w_ffn_down, w_router, b_router, w_moe_up, w_moe_down):
    xp, xs = x_prompt, x_sample
    cmp_p, cmp_s, sel_p, sel_s, win_p, win_s, pool_p, pool_s = [], [], [], [], [], [], [], []
    for i in range(DEPTH):
        j = i // 2
        sm_p, cm_p, gm_p, sf_p, cf_p, gf_p = _adaln(c_prompt, w_ada[i], b_ada[i])
        sm_s, cm_s, gm_s, sf_s, cf_s, gf_s = _adaln(c_sample, w_ada[i], b_ada[i])
        hp = _modulate(xp, sm_p, cm_p)
        hs = _modulate(xs, sm_s, cm_s)
        if i % 2 == 0:
            yp, kc, ksl, kw = _nsa_prompt(hp, w_nsa_in[j], w_cmp1[j], w_cmp2[j], cmp_pos[j], w_nsa_out[j])
            ys, kc2, ksl2, kw2 = _nsa_sample(hs, cache_cmp_kv[j], cache_sel_kv[j], state_win_kv[j], page_table,
                                             w_nsa_in[j], w_cmp1[j], w_cmp2[j], cmp_pos[j], w_nsa_out[j])
            cmp_p.append(kc)
            cmp_s.append(kc2)
            sel_p.append(ksl)
            sel_s.append(ksl2)
            win_p.append(kw)
            win_s.append(kw2)
        else:
            yp = _pool_mix(hp, hp.shape[1], w_pool[j], pool_scale[j])
            hs_ext = jnp.concatenate([state_pool[j], hs], axis=1)
            ys = _pool_mix(hs_ext, hs.shape[1], w_pool[j], pool_scale[j])
            pool_p.append(hp[:, hp.shape[1] - POOL_BUF:])
            pool_s.append(hs_ext[:, hs_ext.shape[1] - POOL_BUF:])
        xp = _post_norm(xp, gm_p * yp, ln_gain[i, 0], ln_bias[i, 0])
        xs = _post_norm(xs, gm_s * ys, ln_gain[i, 0], ln_bias[i, 0])
        hp = _modulate(xp, sf_p, cf_p)
        hs = _modulate(xs, sf_s, cf_s)
        if i % 2 == 0:
            fp = _swiglu(hp, w_ffn_up[j], w_ffn_down[j])
            fs = _swiglu(hs, w_ffn_up[j], w_ffn_down[j])
        else:
            fp = _moe_ffn(hp, w_router[j], b_router[j], w_moe_up[j], w_moe_down[j])
            fs = _moe_ffn(hs, w_router[j], b_router[j], w_moe_up[j], w_moe_down[j])
        xp = _post_norm(xp, gf_p * fp, ln_gain[i, 1], ln_bias[i, 1])
        xs = _post_norm(xs, gf_s * fs, ln_gain[i, 1], ln_bias[i, 1])
    return (xp, xs, jnp.stack(cmp_p), jnp.stack(cmp_s), jnp.stack(sel_p), jnp.stack(sel_s),
            jnp.stack(win_p), jnp.stack(win_s), jnp.stack(pool_p), jnp.stack(pool_s))
```

```python
import functools

import jax
import jax.numpy as jnp
from jax import lax
from jax.experimental import pallas as pl
from jax.experimental.pallas import tpu as pltpu

F32 = jnp.float32
BF16 = jnp.bfloat16

D_MODEL = 2048
BATCH = 8
SEQ = 2048
DEPTH = 2
DEC_BATCH = 128
DEC_SEQ = 8
PAST_LEN = 2048
PAGE_SIZE = 128
N_PAGES = PAST_LEN // PAGE_SIZE
N_HEADS = 16
HEAD_DIM = 128
N_KV_HEADS = 4
GROUP = 4
CMP_BLOCK = 32
CMP_STRIDE = 16
SEL_BLOCK = 64
N_SEL = 16
WINDOW = 512
Q_BLOCK = 128
SCALE = HEAD_DIM ** -0.5
KV_COLS = 2 * N_KV_HEADS * HEAD_DIM
N_Q_COLS = N_HEADS * HEAD_DIM
N_GATE_COLS = N_HEADS * 3
POOL_WINDOWS = (2, 4, 8, 16)
POOL_GROUP = D_MODEL // 4
POOL_BUF = 15
POOL_EXT = 24
D_FF = 11 * D_MODEL // 4
N_EXPERTS = 8
TOP_K = 2
D_FF_EXPERT = 7 * D_MODEL // 2
MOE_BLOCK = 256
ALPHA = (2.0 * DEPTH) ** 0.25
LN_EPS = 1e-5
BIG = 1e9
NEG = -1e30

T_P = BATCH * SEQ
T_S = DEC_BATCH * DEC_SEQ
T_ALL = T_P + T_S
N_CMP = 128
KEY_TILE = 256
VMEM_LIMIT = 56 * 1024 * 1024

_NT = (((1,), (1,)), ((), ()))


def _params(sem, vmem=VMEM_LIMIT):
    return pltpu.CompilerParams(dimension_semantics=sem, vmem_limit_bytes=vmem)


def _layer_norm(z, g, b):
    mu = jnp.mean(z, axis=-1, keepdims=True)
    zc = z - mu
    var = jnp.mean(zc * zc, axis=-1, keepdims=True)
    return zc * lax.rsqrt(var + LN_EPS) * g + b


def _pick(i, n_prompt_tiles, p_ref, s_ref):
    return jnp.where(i < n_prompt_tiles, p_ref[...], s_ref[...])


def _mod_specs(k, tm):
    tpb = SEQ // tm
    npt = T_P // tm
    return (pl.BlockSpec((None, 1, D_MODEL), lambda i, *_: (jnp.minimum(i // tpb, BATCH - 1), 0, k)),
            pl.BlockSpec((tm, D_MODEL), lambda i, *_: (jnp.maximum(i - npt, 0), k)))


def _split_bf16(x):
    hi = x.astype(BF16)
    lo = (x - hi.astype(F32)).astype(BF16)
    return hi, lo


def _ada_kernel(c_ref, w_ref, b_ref, o_ref):
    c = c_ref[...]
    a = (c * jax.nn.sigmoid(c)).astype(BF16)
    o_ref[0] = jnp.dot(a, w_ref[0].astype(BF16), preferred_element_type=F32) + b_ref[0]


def _ada(c_all, w_ada, b_ada):
    nb = c_all.shape[0]
    tn = 1024
    return pl.pallas_call(
        _ada_kernel,
        grid=(DEPTH, 6 * D_MODEL // tn),
        in_specs=[pl.BlockSpec((nb, D_MODEL), lambda l, j: (0, 0)),
                  pl.BlockSpec((1, D_MODEL, tn), lambda l, j: (l, 0, j)),
                  pl.BlockSpec((1, 1, tn), lambda l, j: (l, 0, j))],
        out_specs=pl.BlockSpec((1, nb, tn), lambda l, j: (l, 0, j)),
        out_shape=jax.ShapeDtypeStruct((DEPTH, nb, 6 * D_MODEL), F32),
        compiler_params=_params(("arbitrary", "arbitrary")),
        name="adaln",
    )(c_all, w_ada, b_ada.reshape(DEPTH, 1, 6 * D_MODEL))


def _modulate_kernel(npt, x_ref, shp, shs, scp, scs, h_ref):
    i = pl.program_id(0)
    shift = _pick(i, npt, shp, shs)
    scale = _pick(i, npt, scp, scs)
    h_ref[...] = (x_ref[...] * (1.0 + scale) + shift).astype(BF16)


def _modulate(x, mp, ms, k_shift, k_scale, tm=256):
    row = pl.BlockSpec((tm, D_MODEL), lambda i: (i, 0))
    return pl.pallas_call(
        functools.partial(_modulate_kernel, T_P // tm),
        grid=(T_ALL // tm,),
        in_specs=[row, *_mod_specs(k_shift, tm), *_mod_specs(k_scale, tm)],
        out_specs=row,
        out_shape=jax.ShapeDtypeStruct((T_ALL, D_MODEL), BF16),
        compiler_params=_params(("parallel",)),
        name="modulate",
    )(x, mp, ms, mp, ms)


def _proj_q_kernel(x_ref, w_ref, q_ref):
    q_ref[...] = (jnp.dot(x_ref[...], w_ref[...], preferred_element_type=F32) * SCALE).astype(BF16)


def _proj_kv_kernel(x_ref, w_ref, kv_ref, kvb_ref):
    r = jnp.dot(x_ref[...], w_ref[...], preferred_element_type=F32)
    kv_ref[...] = r
    kvb_ref[...] = r.astype(BF16)


def _proj_gate_kernel(x_ref, w_ref, g_ref):
    g_ref[...] = jax.nn.sigmoid(jnp.dot(x_ref[...], w_ref[...], preferred_element_type=F32))


def _nsa_project(h, w_in_bf, w_gate_bf, tm=1024, tn=1024):
    nm = T_ALL // tm
    x_spec = pl.BlockSpec((tm, D_MODEL), lambda i, j: (i, 0))
    q = pl.pallas_call(
        _proj_q_kernel,
        grid=(nm, N_Q_COLS // tn),
        in_specs=[x_spec, pl.BlockSpec((D_MODEL, tn), lambda i, j: (0, j))],
        out_specs=pl.BlockSpec((tm, tn), lambda i, j: (i, j)),
        out_shape=jax.ShapeDtypeStruct((T_ALL, N_Q_COLS), BF16),
        compiler_params=_params(("parallel", "arbitrary")),
        name="nsa_proj_q",
    )(h, w_in_bf)
    q_blocks = N_Q_COLS // tn
    kv, kvb = pl.pallas_call(
        _proj_kv_kernel,
        grid=(nm, 3 * KV_COLS // tn),
        in_specs=[x_spec, pl.BlockSpec((D_MODEL, tn), lambda i, j: (0, q_blocks + j))],
        out_specs=[pl.BlockSpec((tm, tn), lambda i, j: (i, j))] * 2,
        out_shape=[jax.ShapeDtypeStruct((T_ALL, 3 * KV_COLS), F32),
                   jax.ShapeDtypeStruct((T_ALL, 3 * KV_COLS), BF16)],
        compiler_params=_params(("parallel", "arbitrary")),
        name="nsa_proj_kv",
    )(h, w_in_bf)
    gates = pl.pallas_call(
        _proj_gate_kernel,
        grid=(nm,),
        in_specs=[pl.BlockSpec((tm, D_MODEL), lambda i: (i, 0)),
                  pl.BlockSpec((D_MODEL, 128), lambda i: (0, 0))],
        out_specs=pl.BlockSpec((tm, 128), lambda i: (i, 0)),
        out_shape=jax.ShapeDtypeStruct((T_ALL, 128), F32),
        compiler_params=_params(("parallel",)),
        name="nsa_proj_gate",
    )(h, w_gate_bf)
    return q, kv, kvb, gates


def _gelu_tanh(x):
    return 0.5 * x * (1.0 + jnp.tanh(0.7978845608028654 * (x + 0.044715 * x * x * x)))


def _compress_core(src, wc_ref, w1_ref, w2_ref, cpos_ref, out_ref):
    n_chunk = PAST_LEN // CMP_STRIDE
    for k2 in range(2):
        cols = []
        for i in range(CMP_STRIDE):
            heads = [src[k2 * N_KV_HEADS + h, pl.ds(i, n_chunk, stride=CMP_STRIDE), :]
                     for h in range(N_KV_HEADS)]
            cols.append(jnp.concatenate(heads, axis=0).astype(BF16))
        x = jnp.concatenate(cols, axis=1)
        part = jnp.dot(x, wc_ref[k2], preferred_element_type=F32)
        nrow = N_KV_HEADS * n_chunk
        hsum = part[:, :HEAD_DIM] + pltpu.roll(part[:, HEAD_DIM:], nrow - 1, axis=0)
        pos_bias = jnp.dot(cpos_ref[k2].astype(BF16), w1_ref[k2], preferred_element_type=F32)[0:1]
        hid = _gelu_tanh(hsum + pos_bias)
        kvc = jnp.dot(hid.astype(BF16), w2_ref[k2], preferred_element_type=F32)
        for h in range(N_KV_HEADS):
            c0 = (k2 * N_KV_HEADS + h) * HEAD_DIM
            out_ref[0, :, c0:c0 + HEAD_DIM] = kvc[h * n_chunk:(h + 1) * n_chunk].astype(BF16)


def _compress_prompt_kernel(x_ref, wc_ref, w1_ref, w2_ref, cpos_ref, out_ref, xbuf):
    for c in range(2 * N_KV_HEADS):
        xbuf[c] = x_ref[:, c * HEAD_DIM:(c + 1) * HEAD_DIM]
    _compress_core(xbuf, wc_ref, w1_ref, w2_ref, cpos_ref, out_ref)


def _compress_sample_kernel(pt_ref, *refs):
    pages = refs[:N_PAGES]
    wc_ref, w1_ref, w2_ref, cpos_ref, out_ref, xbuf = refs[N_PAGES:]
    for p in range(N_PAGES):
        for c in range(2 * N_KV_HEADS):
            xbuf[c, p * PAGE_SIZE:(p + 1) * PAGE_SIZE, :] = pages[p][0, :, c * HEAD_DIM:(c + 1) * HEAD_DIM]
    _compress_core(xbuf, wc_ref, w1_ref, w2_ref, cpos_ref, out_ref)


def _compress_weight_specs():
    z3 = lambda *a: (0, 0, 0)
    return [pl.BlockSpec((2, CMP_STRIDE * HEAD_DIM, 2 * HEAD_DIM), z3),
            pl.BlockSpec((2, CMP_BLOCK * HEAD_DIM, HEAD_DIM), z3),
            pl.BlockSpec((2, HEAD_DIM, HEAD_DIM), z3),
            pl.BlockSpec((2, 8, CMP_BLOCK * HEAD_DIM), z3)]


def _compress_prompt(kv_f32, cw):
    return pl.pallas_call(
        _compress_prompt_kernel,
        grid=(BATCH,),
        in_specs=[pl.BlockSpec((SEQ, KV_COLS), lambda b: (b, 0)), *_compress_weight_specs()],
        out_specs=pl.BlockSpec((1, N_CMP, KV_COLS), lambda b: (b, 0, 0)),
        out_shape=jax.ShapeDtypeStruct((BATCH, N_CMP, KV_COLS), BF16),
        scratch_shapes=[pltpu.VMEM((2 * N_KV_HEADS, SEQ, HEAD_DIM), F32)],
        compiler_params=_params(("parallel",)),
        name="compress_prompt",
    )(kv_f32, *cw)


def _page_specs():
    return [pl.BlockSpec((1, PAGE_SIZE, KV_COLS), functools.partial(lambda p, s, pt: (pt[s, p], 0, 0), p))
            for p in range(N_PAGES)]


def _compress_sample(cache, page_table, cw):
    z3 = lambda s, pt: (0, 0, 0)
    grid_spec = pltpu.PrefetchScalarGridSpec(
        num_scalar_prefetch=1,
        grid=(DEC_BATCH,),
        in_specs=[*_page_specs(),
                  pl.BlockSpec((2, CMP_STRIDE * HEAD_DIM, 2 * HEAD_DIM), z3),
                  pl.BlockSpec((2, CMP_BLOCK * HEAD_DIM, HEAD_DIM), z3),
                  pl.BlockSpec((2, HEAD_DIM, HEAD_DIM), z3),
                  pl.BlockSpec((2, 8, CMP_BLOCK * HEAD_DIM), z3)],
        out_specs=pl.BlockSpec((1, N_CMP, KV_COLS), lambda s, pt: (s, 0, 0)),
        scratch_shapes=[pltpu.VMEM((2 * N_KV_HEADS, PAST_LEN, HEAD_DIM), F32)],
    )
    return pl.pallas_call(
        _compress_sample_kernel,
        grid_spec=grid_spec,
        out_shape=jax.ShapeDtypeStruct((DEC_BATCH, N_CMP, KV_COLS), BF16),
        compiler_params=_params(("arbitrary",)),
        name="compress_sample",
    )(page_table, *([cache] * N_PAGES), *cw)


def _masked_exp(s, mask):
    sm = jnp.where(mask, s, NEG)
    m = jnp.max(sm, axis=-1, keepdims=True)
    e = jnp.where(mask, jnp.exp(sm - m), 0.0)
    return e, jnp.sum(e, axis=-1, keepdims=True)


def _rep4(a):
    return jnp.concatenate([a] * GROUP, axis=0)


def _cmp_and_select(qg, kc, vc, ovt, qpos_col, qpos_row, n_tok):
    nsp = ovt.shape[0]
    s = lax.dot_general(qg, kc, _NT, preferred_element_type=F32)
    n_idx = lax.broadcasted_iota(jnp.int32, (n_tok, N_CMP), 1)
    vis = (n_idx * CMP_STRIDE + (CMP_BLOCK - 1) <= qpos_col) & (n_idx < N_CMP - 1)
    e, l = _masked_exp(s, _rep4(vis))
    p = e / jnp.maximum(l, 1e-30)
    o_c = jnp.dot(p.astype(BF16), vc, preferred_element_type=F32)
    psum = p[0:n_tok] + p[n_tok:2 * n_tok] + p[2 * n_tok:3 * n_tok] + p[3 * n_tok:4 * n_tok]
    if n_tok < 128:
        psum = jnp.concatenate([psum, jnp.zeros((128 - n_tok, N_CMP), F32)], axis=0)
    hi, lo = _split_bf16(psum)
    score = (lax.dot_general(ovt, hi, _NT, preferred_element_type=F32)
             + lax.dot_general(ovt, lo, _NT, preferred_element_type=F32))
    m_idx = lax.broadcasted_iota(jnp.int32, (nsp, 128), 0)
    cur = jnp.right_shift(qpos_row, 6)
    forced = (m_idx == 0) | (m_idx == cur) | (m_idx == cur - 1)
    score = jnp.where(forced, BIG, score)
    score = jnp.where(m_idx > cur, -BIG, score)
    rank = jnp.zeros((nsp, 128), F32)
    for mp in range(nsp):
        row = score[mp:mp + 1, :]
        beats = (row > score) | ((row == score) & (m_idx > mp))
        rank = rank + beats.astype(F32)
    sel_t = ((rank < N_SEL) & (score > -0.5 * BIG)).astype(F32)
    sel_t = jnp.concatenate([sel_t, jnp.zeros((128 - nsp, 128), F32)], axis=0)
    return o_c, sel_t.T.astype(BF16)


def _block_expand(n_keys, key0):
    m_idx = lax.broadcasted_iota(jnp.int32, (128, n_keys), 0)
    c_idx = lax.broadcasted_iota(jnp.int32, (128, n_keys), 1)
    return (m_idx == jnp.right_shift(key0 + c_idx, 6)).astype(BF16)


def _flash(qg, k_ref, v_ref, col, lo, hi, mask_fn):
    rows = qg.shape[0]

    def body(kt, carry):
        m, l, acc = carry
        r0 = pl.multiple_of(kt * KEY_TILE, KEY_TILE)
        k = k_ref[pl.ds(r0, KEY_TILE), col:col + HEAD_DIM]
        v = v_ref[pl.ds(r0, KEY_TILE), N_KV_HEADS * HEAD_DIM + col:N_KV_HEADS * HEAD_DIM + col + HEAD_DIM]
        s = lax.dot_general(qg, k, _NT, preferred_element_type=F32)
        mask = _rep4(mask_fn(kt * KEY_TILE))
        sm = jnp.where(mask, s, NEG)
        m_new = jnp.maximum(m, jnp.max(sm, axis=-1, keepdims=True))
        alpha = jnp.exp(m - m_new)
        e = jnp.where(mask, jnp.exp(sm - m_new), 0.0)
        l = alpha * l + jnp.sum(e, axis=-1, keepdims=True)
        acc = alpha * acc + jnp.dot(e.astype(BF16), v, preferred_element_type=F32)
        return m_new, l, acc

    init = (jnp.full((rows, 1), NEG, F32), jnp.zeros((rows, 1), F32), jnp.zeros((rows, HEAD_DIM), F32))
    _, l, acc = lax.fori_loop(lo, hi, body, init)
    return acc / jnp.maximum(l, 1e-30)


def _combine_heads(g, n_tok, o_c, o_s, o_w, gates, o_ref):
    for j in range(GROUP):
        hd = g * GROUP + j
        rows = slice(j * n_tok, (j + 1) * n_tok)
        o = (gates[:, 3 * hd:3 * hd + 1] * o_c[rows] + gates[:, 3 * hd + 1:3 * hd + 2] * o_s[rows]
             + gates[:, 3 * hd + 2:3 * hd + 3] * o_w[rows])
        o_ref[:, hd * HEAD_DIM:(hd + 1) * HEAD_DIM] = o.astype(o_ref.dtype)


def _attn_prompt_kernel(q_ref, ks_ref, kw_ref, kvc_ref, gate_ref, ovt_ref, o_ref):
    qb = pl.program_id(1)
    q0 = qb * Q_BLOCK
    qpos_col = q0 + lax.broadcasted_iota(jnp.int32, (Q_BLOCK, 1), 0)
    qpos_row = q0 + lax.broadcasted_iota(jnp.int32, (1, Q_BLOCK), 1)
    gates = gate_ref[...]
    ovt = ovt_ref[...]
    last_tile = qb // 2 + 1
    c_idx = lax.broadcasted_iota(jnp.int32, (Q_BLOCK, KEY_TILE), 1)
    for g in range(N_KV_HEADS):
        col = g * HEAD_DIM
        qg = jnp.concatenate([q_ref[:, (g * GROUP + j) * HEAD_DIM:(g * GROUP + j + 1) * HEAD_DIM]
                              for j in range(GROUP)], axis=0)
        o_c, sel = _cmp_and_select(qg, kvc_ref[0, :, col:col + HEAD_DIM],
                                   kvc_ref[0, :, N_KV_HEADS * HEAD_DIM + col:N_KV_HEADS * HEAD_DIM + col + HEAD_DIM],
                                   ovt, qpos_col, qpos_row, Q_BLOCK)

        def sel_mask(key0, sel=sel):
            chosen = jnp.dot(sel, _block_expand(KEY_TILE, key0), preferred_element_type=F32) > 0.5
            return chosen & (key0 + c_idx <= qpos_col)

        def win_mask(key0):
            dist = qpos_col - (key0 + c_idx)
            return (dist >= 0) & (dist <= WINDOW)

        o_s = _flash(qg, ks_ref, ks_ref, col, 0, last_tile, sel_mask)
        o_w = _flash(qg, kw_ref, kw_ref, col, jnp.maximum(qb // 2 - 2, 0), last_tile, win_mask)
        _combine_heads(g, Q_BLOCK, o_c, o_s, o_w, gates, o_ref)


def _attn_prompt(q, kvb, kvc, gates, ovt):
    nqb = SEQ // Q_BLOCK
    return pl.pallas_call(
        _attn_prompt_kernel,
        grid=(BATCH, nqb),
        in_specs=[pl.BlockSpec((Q_BLOCK, N_Q_COLS), lambda b, i: (b * nqb + i, 0)),
                  pl.BlockSpec((SEQ, KV_COLS), lambda b, i: (b, 1)),
                  pl.BlockSpec((SEQ, KV_COLS), lambda b, i: (b, 2)),
                  pl.BlockSpec((1, N_CMP, KV_COLS), lambda b, i: (b, 0, 0)),
                  pl.BlockSpec((Q_BLOCK, 128), lambda b, i: (b * nqb + i, 0)),
                  pl.BlockSpec(ovt.shape, lambda b, i: (0, 0))],
        out_specs=pl.BlockSpec((Q_BLOCK, N_Q_COLS), lambda b, i: (b * nqb + i, 0)),
        out_shape=jax.ShapeDtypeStruct((T_P, N_Q_COLS), BF16),
        compiler_params=_params(("parallel", "arbitrary")),
        name="attn_prompt",
    )(q, kvb, kvb, kvc, gates, ovt)


TAIL = 128
SEL_KEYS = PAST_LEN + TAIL
WIN_KEYS = WINDOW + TAIL


def _attn_sample_kernel(pt_ref, *refs):
    pages = refs[:N_PAGES]
    q_ref, new_ref, win_ref, kvc_ref, gate_ref, ovt_ref, o_ref, sbuf, wbuf = refs[N_PAGES:]
    for p in range(N_PAGES):
        sbuf[p * PAGE_SIZE:(p + 1) * PAGE_SIZE, :] = pages[p][0].astype(BF16)
    pad = jnp.zeros((TAIL - DEC_SEQ, KV_COLS), F32)
    sbuf[PAST_LEN:, :] = jnp.concatenate([new_ref[:, KV_COLS:2 * KV_COLS], pad], axis=0).astype(BF16)
    wbuf[:WINDOW, :] = win_ref[0].astype(BF16)
    wbuf[WINDOW:, :] = jnp.concatenate([new_ref[:, 2 * KV_COLS:], pad], axis=0).astype(BF16)

    tcol = lax.broadcasted_iota(jnp.int32, (DEC_SEQ, 1), 0)
    qpos_col = PAST_LEN + tcol
    qpos_row = PAST_LEN + jnp.minimum(lax.broadcasted_iota(jnp.int32, (1, 128), 1), DEC_SEQ - 1)
    gates = gate_ref[...]
    ovt = ovt_ref[...]
    expand = _block_expand(SEL_KEYS, 0)
    sel_pos = lax.broadcasted_iota(jnp.int32, (DEC_SEQ, SEL_KEYS), 1)
    win_c = lax.broadcasted_iota(jnp.int32, (DEC_SEQ, WIN_KEYS), 1)
    win_mask = _rep4((win_c >= tcol) & (win_c <= WINDOW + tcol))
    for g in range(N_KV_HEADS):
        col = g * HEAD_DIM
        vcol = N_KV_HEADS * HEAD_DIM + col
        qg = jnp.concatenate([q_ref[:, (g * GROUP + j) * HEAD_DIM:(g * GROUP + j + 1) * HEAD_DIM]
                              for j in range(GROUP)], axis=0).astype(BF16)
        o_c, sel = _cmp_and_select(qg, kvc_ref[0, :, col:col + HEAD_DIM], kvc_ref[0, :, vcol:vcol + HEAD_DIM],
                                   ovt, qpos_col, qpos_row, DEC_SEQ)
        chosen = jnp.dot(sel, expand, preferred_element_type=F32)[:DEC_SEQ] > 0.5
        mask = _rep4(chosen & (sel_pos <= qpos_col))
        s = lax.dot_general(qg, sbuf[:, col:col + HEAD_DIM], _NT, preferred_element_type=F32)
        e, l = _masked_exp(s, mask)
        o_s = jnp.dot(e.astype(BF16), sbuf[:, vcol:vcol + HEAD_DIM], preferred_element_type=F32)
        o_s = o_s / jnp.maximum(l, 1e-30)
        s = lax.dot_general(qg, wbuf[:, col:col + HEAD_DIM], _NT, preferred_element_type=F32)
        e, l = _masked_exp(s, win_mask)
        o_w = jnp.dot(e.astype(BF16), wbuf[:, vcol:vcol + HEAD_DIM], preferred_element_type=F32)
        o_w = o_w / jnp.maximum(l, 1e-30)
        _combine_heads(g, DEC_SEQ, o_c, o_s, o_w, gates, o_ref)


def _attn_sample(q_s, kv_f32, cache_sel, state_win, page_table, kvc, gates, ovt):
    row0 = T_P // DEC_SEQ
    grid_spec = pltpu.PrefetchScalarGridSpec(
        num_scalar_prefetch=1,
        grid=(DEC_BATCH,),
        in_specs=[*_page_specs(),
                  pl.BlockSpec((DEC_SEQ, N_Q_COLS), lambda s, pt: (s, 0)),
                  pl.BlockSpec((DEC_SEQ, 3 * KV_COLS), lambda s, pt: (row0 + s, 0)),
                  pl.BlockSpec((1, WINDOW, KV_COLS), lambda s, pt: (s, 0, 0)),
                  pl.BlockSpec((1, N_CMP, KV_COLS), lambda s, pt: (s, 0, 0)),
                  pl.BlockSpec((DEC_SEQ, 128), lambda s, pt: (row0 + s, 0)),
                  pl.BlockSpec(ovt.shape, lambda s, pt: (0, 0))],
        out_specs=pl.BlockSpec((DEC_SEQ, N_Q_COLS), lambda s, pt: (s, 0)),
        scratch_shapes=[pltpu.VMEM((SEL_KEYS, KV_COLS), BF16), pltpu.VMEM((WIN_KEYS, KV_COLS), BF16)],
    )
    return pl.pallas_call(
        _attn_sample_kernel,
        grid_spec=grid_spec,
        out_shape=jax.ShapeDtypeStruct((T_S, N_Q_COLS), F32),
        compiler_params=_params(("arbitrary",)),
        name="attn_sample",
    )(page_table, *([cache_sel] * N_PAGES), q_s, kv_f32, state_win, kvc, gates, ovt)


def _overlap_t(ns, nsp):
    c0 = jnp.arange(N_CMP)[None, :] * CMP_STRIDE
    s0 = jnp.arange(nsp)[:, None] * SEL_BLOCK
    ov = jnp.minimum(c0 + CMP_BLOCK, s0 + SEL_BLOCK) - jnp.maximum(c0, s0)
    ov = jnp.maximum(ov, 0).astype(F32) / CMP_BLOCK
    ov = jnp.where((jnp.arange(nsp)[:, None] < ns) & (jnp.arange(N_CMP)[None, :] < N_CMP - 1), ov, 0.0)
    return ov.astype(BF16)


def _outproj_kernel(npt, o_ref, w_ref, x_ref, gp, gs, shp, shs, scp, scs, lng, lnb, x1_ref, h1_ref):
    i = pl.program_id(0)
    y = jnp.dot(o_ref[...], w_ref[...], preferred_element_type=F32)
    x1 = _layer_norm(ALPHA * x_ref[...] + _pick(i, npt, gp, gs) * y, lng[...], lnb[...])
    x1_ref[...] = x1
    h1_ref[...] = (x1 * (1.0 + _pick(i, npt, scp, scs)) + _pick(i, npt, shp, shs)).astype(BF16)


def _outproj(o, w_out_bf, x, mp, ms, lng, lnb, tm=256):
    row = pl.BlockSpec((tm, D_MODEL), lambda i: (i, 0))
    vec = pl.BlockSpec((1, D_MODEL), lambda i: (0, 0))
    return pl.pallas_call(
        functools.partial(_outproj_kernel, T_P // tm),
        grid=(T_ALL // tm,),
        in_specs=[row, pl.BlockSpec((N_Q_COLS, D_MODEL), lambda i: (0, 0)), row,
                  *_mod_specs(2, tm), *_mod_specs(3, tm), *_mod_specs(4, tm), vec, vec],
        out_specs=[row, row],
        out_shape=[jax.ShapeDtypeStruct((T_ALL, D_MODEL), F32), jax.ShapeDtypeStruct((T_ALL, D_MODEL), BF16)],
        compiler_params=_params(("parallel",)),
        name="nsa_out_postnorm",
    )(o, w_out_bf, x, mp, ms, mp, ms, mp, ms, lng, lnb)


def _swiglu_up_kernel(x_ref, wg_ref, wu_ref, a_ref):
    x = x_ref[...]
    g = jnp.dot(x, wg_ref[...], preferred_element_type=F32)
    u = jnp.dot(x, wu_ref[...], preferred_element_type=F32)
    a_ref[...] = (g * jax.nn.sigmoid(g) * u).astype(BF16)


def _ffn_up(h, w_up_bf, tm=1024, tn=512):
    nj = D_FF // tn
    return pl.pallas_call(
        _swiglu_up_kernel,
        grid=(T_ALL // tm, nj),
        in_specs=[pl.BlockSpec((tm, D_MODEL), lambda i, j: (i, 0)),
                  pl.BlockSpec((D_MODEL, tn), lambda i, j: (0, j)),
                  pl.BlockSpec((D_MODEL, tn), lambda i, j: (0, nj + j))],
        out_specs=pl.BlockSpec((tm, tn), lambda i, j: (i, j)),
        out_shape=jax.ShapeDtypeStruct((T_ALL, D_FF), BF16),
        compiler_params=_params(("parallel", "arbitrary")),
        name="ffn_up",
    )(h, w_up_bf, w_up_bf)


def _ffn_down_kernel(npt, a_ref, w_ref, x_ref, gp, gs, lng, lnb, x2_ref, acc):
    i = pl.program_id(0)
    k = pl.program_id(1)

    @pl.when(k == 0)
    def _():
        acc[...] = jnp.zeros_like(acc)

    acc[...] += jnp.dot(a_ref[...], w_ref[...], preferred_element_type=F32)

    @pl.when(k == pl.num_programs(1) - 1)
    def _():
        x2_ref[...] = _layer_norm(ALPHA * x_ref[...] + _pick(i, npt, gp, gs) * acc[...], lng[...], lnb[...])


def _ffn_down(a, w_down_bf, x, mp, ms, lng, lnb, tm=512, tk=512):
    row = pl.BlockSpec((tm, D_MODEL), lambda i, k: (i, 0))
    vec = pl.BlockSpec((1, D_MODEL), lambda i, k: (0, 0))
    return pl.pallas_call(
        functools.partial(_ffn_down_kernel, T_P // tm),
        grid=(T_ALL // tm, D_FF // tk),
        in_specs=[pl.BlockSpec((tm, tk), lambda i, k: (i, k)),
                  pl.BlockSpec((tk, D_MODEL), lambda i, k: (k, 0)),
                  row, *_mod_specs(5, tm), vec, vec],
        out_specs=row,
        out_shape=jax.ShapeDtypeStruct((T_ALL, D_MODEL), F32),
        scratch_shapes=[pltpu.VMEM((tm, D_MODEL), F32)],
        compiler_params=_params(("parallel", "arbitrary")),
        name="ffn_down_postnorm",
    )(a, w_down_bf, x, mp, ms, lng, lnb)


HALO = 16


def _pool_kernel(prompt, tiles_per_seq, x_ref, halo_ref, sh_m, sc_m, g_m, sh_f, sc_f, hsh, hsc,
                 wp_ref, ps_ref, lng, lnb, wr_hi, wr_lo, br_ref, x1_ref, h1_ref, hm_ref, lg_ref):
    i = pl.program_id(0)
    tm = x_ref.shape[0]
    x = x_ref[...]
    h = x * (1.0 + sc_m[...]) + sh_m[...]
    hm_ref[...] = h
    if prompt:
        hh = halo_ref[...] * (1.0 + hsc[...]) + hsh[...]
        hh = jnp.where(i % tiles_per_seq == 0, 0.0, hh)
        pos = (i % tiles_per_seq) * tm + lax.broadcasted_iota(jnp.int32, (tm, 1), 0)
    else:
        hh = jnp.zeros((HALO, D_MODEL), F32)
    ext = jnp.concatenate([hh, h], axis=0)
    ys = []
    for gi, w in enumerate(POOL_WINDOWS):
        a = ext[:, gi * POOL_GROUP:(gi + 1) * POOL_GROUP]
        s = a
        d = 1
        while d < w:
            s = s + pltpu.roll(s, d, axis=0)
            d *= 2
        s = s[HALO:]
        if prompt:
            inv = 1.0 / jnp.minimum(pos + 1, w).astype(F32)
        else:
            inv = 1.0 / w
        diff = (s * inv - a[HALO:]).astype(BF16)
        ys.append(jnp.dot(diff, wp_ref[gi], preferred_element_type=F32))
    y = jnp.concatenate(ys, axis=1) * ps_ref[...]
    x1 = _layer_norm(ALPHA * x + g_m[...] * y, lng[...], lnb[...])
    x1_ref[...] = x1
    h1 = x1 * (1.0 + sc_f[...]) + sh_f[...]
    h1_ref[...] = h1.astype(BF16)
    hi, lo = _split_bf16(h1)
    lg_ref[...] = (jnp.dot(hi, wr_hi[...], preferred_element_type=F32)
                   + jnp.dot(hi, wr_lo[...], preferred_element_type=F32)
                   + jnp.dot(lo, wr_hi[...], preferred_element_type=F32) + br_ref[...])


def _pool_call(prompt, x, n, mods, w_pool_bf, pool_scale, lng, lnb, wr_hi, wr_lo, b_router, tm):
    row = pl.BlockSpec((tm, D_MODEL), lambda i: (i, 0))
    vec = pl.BlockSpec((1, D_MODEL), lambda i: (0, 0))
    halo = pl.BlockSpec((HALO, D_MODEL), lambda i: (jnp.maximum(i * (tm // HALO) - 1, 0), 0))
    specs = [s for s, _ in mods]
    arrs = [a for _, a in mods]
    return pl.pallas_call(
        functools.partial(_pool_kernel, prompt, SEQ // tm),
        grid=(n // tm,),
        in_specs=[row, halo, *specs,
                  pl.BlockSpec((4, POOL_GROUP, POOL_GROUP), lambda i: (0, 0, 0)), vec, vec, vec,
                  pl.BlockSpec((D_MODEL, 128), lambda i: (0, 0)), pl.BlockSpec((D_MODEL, 128), lambda i: (0, 0)),
                  pl.BlockSpec((1, 128), lambda i: (0, 0))],
        out_specs=[row, row, row, pl.BlockSpec((tm, 128), lambda i: (i, 0))],
        out_shape=[jax.ShapeDtypeStruct((n, D_MODEL), F32), jax.ShapeDtypeStruct((n, D_MODEL), BF16),
                   jax.ShapeDtypeStruct((n, D_MODEL), F32), jax.ShapeDtypeStruct((n, 128), F32)],
        compiler_params=_params(("parallel",)),
        name="pool_prompt" if prompt else "pool_sample",
    )(x, x, *arrs, w_pool_bf, pool_scale, lng, lnb, wr_hi, wr_lo, b_router)


_POOL_MOD_CHUNKS = (0, 1, 2, 3, 4, 0, 1)


def _pool_prompt(x, mp, *rest, tm=256):
    tpb = SEQ // tm
    mods = [(pl.BlockSpec((None, 1, D_MODEL), functools.partial(lambda k, i: (i // tpb, 0, k), k)), mp)
            for k in _POOL_MOD_CHUNKS]
    return _pool_call(True, x, T_P, mods, *rest, tm)


def _pool_sample(x_ext, ms_ext, *rest, tm=8 * POOL_EXT):
    mods = [(pl.BlockSpec((tm, D_MODEL), functools.partial(lambda k, i: (i, k), k)), ms_ext)
            for k in _POOL_MOD_CHUNKS[:5]]
    mods += [(pl.BlockSpec((HALO, D_MODEL), functools.partial(lambda k, i: (0, k), k)), ms_ext)
             for k in _POOL_MOD_CHUNKS[5:]]
    return _pool_call(False, x_ext, DEC_BATCH * POOL_EXT, mods, *rest, tm)


def _moe_up_kernel(be_ref, x_ref, wg_ref, wu_ref, a_ref):
    x = x_ref[...]
    g = jnp.dot(x, wg_ref[...], preferred_element_type=F32)
    u = jnp.dot(x, wu_ref[...], preferred_element_type=F32)
    a_ref[...] = (g * jax.nn.sigmoid(g) * u).astype(BF16)


def _moe_up(xb, w_up_bf, block_expert, tn=1024):
    n_slots = xb.shape[0]
    nj = D_FF_EXPERT // tn
    grid_spec = pltpu.PrefetchScalarGridSpec(
        num_scalar_prefetch=1,
        grid=(nj, n_slots // MOE_BLOCK),
        in_specs=[pl.BlockSpec((MOE_BLOCK, D_MODEL), lambda j, i, be: (i, 0)),
                  pl.BlockSpec((None, D_MODEL, tn), lambda j, i, be: (be[i], 0, j)),
                  pl.BlockSpec((None, D_MODEL, tn), lambda j, i, be: (be[i], 0, nj + j))],
        out_specs=pl.BlockSpec((MOE_BLOCK, tn), lambda j, i, be: (i, j)),
    )
    return pl.pallas_call(
        _moe_up_kernel,
        grid_spec=grid_spec,
        out_shape=jax.ShapeDtypeStruct((n_slots, D_FF_EXPERT), BF16),
        compiler_params=_params(("arbitrary", "arbitrary")),
        name="moe_up",
    )(block_expert, xb, w_up_bf, w_up_bf)


def _moe_down_kernel(be_ref, a_ref, w_ref, sg_ref, y_ref):
    y_ref[...] = jnp.dot(a_ref[...], w_ref[...], preferred_element_type=F32) * sg_ref[...]


def _moe_down(act, w_down_bf, slot_gate, block_expert, tn=512):
    n_slots = act.shape[0]
    grid_spec = pltpu.PrefetchScalarGridSpec(
        num_scalar_prefetch=1,
        grid=(D_MODEL // tn, n_slots // MOE_BLOCK),
        in_specs=[pl.BlockSpec((MOE_BLOCK, D_FF_EXPERT), lambda j, i, be: (i, 0)),
                  pl.BlockSpec((None, D_FF_EXPERT, tn), lambda j, i, be: (be[i], 0, j)),
                  pl.BlockSpec((MOE_BLOCK, 1), lambda j, i, be: (i, 0))],
        out_specs=pl.BlockSpec((MOE_BLOCK, tn), lambda j, i, be: (i, j)),
    )
    return pl.pallas_call(
        _moe_down_kernel,
        grid_spec=grid_spec,
        out_shape=jax.ShapeDtypeStruct((n_slots, D_MODEL), F32),
        compiler_params=_params(("arbitrary", "arbitrary")),
        name="moe_down",
    )(block_expert, act, w_down_bf, slot_gate)


def _route(logits):
    n = logits.shape[0]
    top_logit, top_e = lax.top_k(logits, TOP_K)
    gate = jax.nn.softmax(top_logit, axis=-1)
    e_flat = top_e.reshape(-1)
    tok_flat = jnp.repeat(jnp.arange(n, dtype=jnp.int32), TOP_K)
    order = jnp.argsort(e_flat, stable=True)
    e_sorted = e_flat[order]
    counts = jnp.bincount(e_flat, length=N_EXPERTS)
    padded = (counts + MOE_BLOCK - 1) // MOE_BLOCK * MOE_BLOCK
    start = jnp.cumsum(counts) - counts
    pstart = jnp.cumsum(padded) - padded
    dest = (pstart[e_sorted] + jnp.arange(n * TOP_K) - start[e_sorted]).astype(jnp.int32)
    n_blocks = -(-(n * TOP_K) // MOE_BLOCK) + N_EXPERTS
    n_slots = n_blocks * MOE_BLOCK
    slot_tok = jnp.full((n_slots,), n, jnp.int32).at[dest].set(tok_flat[order])
    slot_gate = jnp.zeros((n_slots,), F32).at[dest].set(gate.reshape(-1)[order])
    block_expert = jnp.minimum(
        jnp.searchsorted(jnp.cumsum(padded), jnp.arange(n_blocks) * MOE_BLOCK, side='right'),
        N_EXPERTS - 1).astype(jnp.int32)
    slot_of = jnp.zeros((n * TOP_K,), jnp.int32).at[order].set(dest).reshape(n, TOP_K)
    return slot_tok, slot_gate, block_expert, slot_of


def _final_kernel(npt, x_ref, y_ref, gp, gs, lng, lnb, o_ref):
    i = pl.program_id(0)
    o_ref[...] = _layer_norm(ALPHA * x_ref[...] + _pick(i, npt, gp, gs) * y_ref[...], lng[...], lnb[...])


def _final(x, y, mp, ms, lng, lnb, tm=256):
    row = pl.BlockSpec((tm, D_MODEL), lambda i: (i, 0))
    vec = pl.BlockSpec((1, D_MODEL), lambda i: (0, 0))
    return pl.pallas_call(
        functools.partial(_final_kernel, T_P // tm),
        grid=(T_ALL // tm,),
        in_specs=[row, row, *_mod_specs(5, tm), vec, vec],
        out_specs=row,
        out_shape=jax.ShapeDtypeStruct((T_ALL, D_MODEL), F32),
        compiler_params=_params(("parallel",)),
        name="final_postnorm",
    )(x, y, mp, ms, lng, lnb)


def kernel(x_prompt, x_sample, cache_cmp_kv, cache_sel_kv, state_win_kv, state_pool, page_table,
           c_prompt, c_sample, w_ada, b_ada, ln_gain, ln_bias, w_nsa_in, w_cmp1, w_cmp2, cmp_pos,
           w_nsa_out, w_pool, pool_scale, w_ffn_up, w_ffn_down, w_router, b_router, w_moe_up, w_moe_down):
    x = jnp.concatenate([x_prompt.reshape(T_P, D_MODEL), x_sample.reshape(T_S, D_MODEL)], axis=0)
    mods = _ada(jnp.concatenate([c_prompt, c_sample], axis=0), w_ada, b_ada)

    def layer_mods(i):
        return mods[i, :BATCH].reshape(BATCH, 1, 6 * D_MODEL), jnp.repeat(mods[i, BATCH:], DEC_SEQ, axis=0)

    mp, ms = layer_mods(0)
    lng = ln_gain[0].reshape(2, 1, D_MODEL)
    lnb = ln_bias[0].reshape(2, 1, D_MODEL)
    h = _modulate(x, mp, ms, 0, 1)
    w_in_bf = w_nsa_in[0].astype(BF16)
    w_gate_bf = jnp.pad(w_in_bf[:, N_Q_COLS + 3 * KV_COLS:], ((0, 0), (0, 128 - N_GATE_COLS)))
    q, kv, kvb, gates = _nsa_project(h, w_in_bf, w_gate_bf)

    w1 = w_cmp1[0].astype(BF16)
    wc = w1.reshape(2, 2, CMP_STRIDE, HEAD_DIM, HEAD_DIM).transpose(0, 2, 3, 1, 4).reshape(
        2, CMP_STRIDE * HEAD_DIM, 2 * HEAD_DIM)
    cpos = jnp.pad(cmp_pos[0].reshape(2, 1, CMP_BLOCK * HEAD_DIM), ((0, 0), (0, 7), (0, 0)))
    cw = (wc, w1, w_cmp2[0].astype(BF16), cpos)
    n_phys = cache_cmp_kv.shape[1]
    kvc_p = _compress_prompt(kv, cw)
    kvc_s = _compress_sample(cache_cmp_kv[0].reshape(n_phys, PAGE_SIZE, KV_COLS), page_table, cw)

    o_p = _attn_prompt(q, kvb, kvc_p, gates, _overlap_t(SEQ // SEL_BLOCK, 32))
    ns_s = -(-(PAST_LEN + DEC_SEQ) // SEL_BLOCK)
    o_s = _attn_sample(q[T_P:].astype(F32), kv, cache_sel_kv[0].reshape(n_phys, PAGE_SIZE, KV_COLS),
                       state_win_kv[0].reshape(DEC_BATCH, WINDOW, KV_COLS), page_table, kvc_s, gates,
                       _overlap_t(ns_s, 40))
    o = jnp.concatenate([o_p, o_s.astype(BF16)], axis=0)
    x1, h1 = _outproj(o, w_nsa_out[0].astype(BF16), x, mp, ms, lng[0], lnb[0])
    act = _ffn_up(h1, w_ffn_up[0].astype(BF16))
    x2 = _ffn_down(act, w_ffn_down[0].astype(BF16), x1, mp, ms, lng[1], lnb[1])

    kv6 = lambda a, n, t: a.reshape(1, n, t, 2, N_KV_HEADS, HEAD_DIM)
    kv_p = kv[:T_P].reshape(BATCH, SEQ, 3, KV_COLS)
    kv_s = kv[T_P:].reshape(DEC_BATCH, DEC_SEQ, 3, KV_COLS)
    new_cmp_p = kv6(kv_p[:, :, 0], BATCH, SEQ)
    new_sel_p = kv6(kv_p[:, :, 1], BATCH, SEQ)
    new_win_p = kv6(kv_p[:, SEQ - WINDOW:, 2], BATCH, WINDOW)
    new_cmp_s = kv6(kv_s[:, :, 0], DEC_BATCH, DEC_SEQ)
    new_sel_s = kv6(kv_s[:, :, 1], DEC_BATCH, DEC_SEQ)
    new_win_s = jnp.concatenate(
        [state_win_kv[:, :, DEC_SEQ:], kv6(kv_s[:, :, 2], DEC_BATCH, DEC_SEQ)], axis=2)

    mp, ms = layer_mods(1)
    lng = ln_gain[1].reshape(2, 1, D_MODEL)
    lnb = ln_bias[1].reshape(2, 1, D_MODEL)
    wr_pad = jnp.pad(w_router[0], ((0, 0), (0, 128 - N_EXPERTS)))
    wr_hi, wr_lo = _split_bf16(wr_pad)
    br = jnp.pad(b_router[0].reshape(1, N_EXPERTS), ((0, 0), (0, 128 - N_EXPERTS)))
    w_pool_bf = w_pool[0].astype(BF16)
    ps = pool_scale[0].reshape(1, D_MODEL)

    pool_args = (w_pool_bf, ps, lng[0], lnb[0], wr_hi, wr_lo, br)
    x1p, h1p, hmp, lgp = _pool_prompt(x2, mp, *pool_args)

    ms3 = ms.reshape(DEC_BATCH, DEC_SEQ, 6 * D_MODEL)
    ms_ext = jnp.pad(ms3, ((0, 0), (POOL_EXT - DEC_SEQ, 0), (0, 0))).reshape(DEC_BATCH * POOL_EXT, 6 * D_MODEL)
    xs_ext = jnp.concatenate([jnp.zeros((DEC_BATCH, 1, D_MODEL), F32), state_pool[0],
                              x2[T_P:].reshape(DEC_BATCH, DEC_SEQ, D_MODEL)], axis=1)
    xs_ext = xs_ext.reshape(DEC_BATCH * POOL_EXT, D_MODEL)
    x1s, h1s, hms, lgs = _pool_sample(xs_ext, ms_ext, *pool_args)
    new_rows = lambda a: a.reshape(DEC_BATCH, POOL_EXT, -1)[:, POOL_EXT - DEC_SEQ:].reshape(T_S, -1)
    x3 = jnp.concatenate([x1p, new_rows(x1s)], axis=0)
    h3 = jnp.concatenate([h1p, new_rows(h1s)], axis=0)
    logits = jnp.concatenate([lgp, new_rows(lgs)], axis=0)[:, :N_EXPERTS]
    new_pool_p = hmp.reshape(BATCH, SEQ, D_MODEL)[None, :, SEQ - POOL_BUF:]
    new_pool_s = hms.reshape(DEC_BATCH, POOL_EXT, D_MODEL)[None, :, POOL_EXT - POOL_BUF:]

    slot_tok, slot_gate, block_expert, slot_of = _route(logits)
    h3_pad = jnp.concatenate([h3, jnp.zeros((1, D_MODEL), BF16)], axis=0)
    xb = h3_pad[slot_tok]
    act = _moe_up(xb, w_moe_up[0].astype(BF16), block_expert)
    yb = _moe_down(act, w_moe_down[0].astype(BF16), slot_gate.reshape(-1, 1), block_expert)
    y = yb[slot_of[:, 0]] + yb[slot_of[:, 1]]
    x4 = _final(x3, y, mp, ms, lng[1], lnb[1])

    y_prompt = x4[:T_P].reshape(BATCH, SEQ, D_MODEL)
    y_sample = x4[T_P:].reshape(DEC_BATCH, DEC_SEQ, D_MODEL)
    return (y_prompt, y_sample, new_cmp_p, new_cmp_s, new_sel_p, new_sel_s, new_win_p, new_win_s,
            new_pool_p, new_pool_s)
```

```python
import functools

import jax
import jax.numpy as jnp
from jax import lax
from jax.experimental import pallas as pl
from jax.experimental.pallas import tpu as pltpu

F32 = jnp.float32
BF16 = jnp.bfloat16

D_MODEL = 2048
BATCH = 8
SEQ = 2048
DEPTH = 2
DEC_BATCH = 128
DEC_SEQ = 8
PAST_LEN = 2048
PAGE_SIZE = 128
N_PAGES = PAST_LEN // PAGE_SIZE
N_HEADS = 16
HEAD_DIM = 128
N_KV_HEADS = 4
GROUP = 4
CMP_BLOCK = 32
CMP_STRIDE = 16
SEL_BLOCK = 64
N_SEL = 16
WINDOW = 512
Q_BLOCK = 128
SCALE = HEAD_DIM ** -0.5
KV_COLS = 2 * N_KV_HEADS * HEAD_DIM
ROW_SPLIT = 2 * N_KV_HEADS
N_Q_COLS = N_HEADS * HEAD_DIM
N_GATE_COLS = N_HEADS * 3
POOL_WINDOWS = (2, 4, 8, 16)
POOL_GROUP = D_MODEL // 4
POOL_BUF = 15
POOL_EXT = 24
D_FF = 11 * D_MODEL // 4
N_EXPERTS = 8
TOP_K = 2
D_FF_EXPERT = 7 * D_MODEL // 2
MOE_BLOCK = 256
ALPHA = (2.0 * DEPTH) ** 0.25
LN_EPS = 1e-5
BIG = 1e9
NEG = -1e30

T_P = BATCH * SEQ
T_S = DEC_BATCH * DEC_SEQ
T_ALL = T_P + T_S
N_CMP = 128
KEY_TILE = 256
VMEM_LIMIT = 56 * 1024 * 1024

_NT = (((1,), (1,)), ((), ()))


def _params(sem, vmem=VMEM_LIMIT):
    return pltpu.CompilerParams(dimension_semantics=sem, vmem_limit_bytes=vmem)


def _layer_norm(z, g, b):
    mu = jnp.mean(z, axis=-1, keepdims=True)
    zc = z - mu
    var = jnp.mean(zc * zc, axis=-1, keepdims=True)
    return zc * lax.rsqrt(var + LN_EPS) * g + b


def _pick(i, n_prompt_tiles, p_ref, s_ref):
    return jnp.where(i < n_prompt_tiles, p_ref[...], s_ref[...])


def _mod_specs(k, tm):
    tpb = SEQ // tm
    npt = T_P // tm
    return (pl.BlockSpec((None, 1, D_MODEL), lambda i, *_: (jnp.minimum(i // tpb, BATCH - 1), 0, k)),
            pl.BlockSpec((tm, D_MODEL), lambda i, *_: (jnp.maximum(i - npt, 0), k)))


def _split_bf16(x):
    hi = x.astype(BF16)
    lo = (x - hi.astype(F32)).astype(BF16)
    return hi, lo


def _ada_kernel(c_ref, w_ref, b_ref, o_ref):
    c = c_ref[...]
    a = (c * jax.nn.sigmoid(c)).astype(BF16)
    o_ref[0] = jnp.dot(a, w_ref[0].astype(BF16), preferred_element_type=F32) + b_ref[0]


def _ada(c_all, w_ada, b_ada):
    nb = c_all.shape[0]
    tn = 1024
    return pl.pallas_call(
        _ada_kernel,
        grid=(DEPTH, 6 * D_MODEL // tn),
        in_specs=[pl.BlockSpec((nb, D_MODEL), lambda l, j: (0, 0)),
                  pl.BlockSpec((1, D_MODEL, tn), lambda l, j: (l, 0, j)),
                  pl.BlockSpec((1, 1, tn), lambda l, j: (l, 0, j))],
        out_specs=pl.BlockSpec((1, nb, tn), lambda l, j: (l, 0, j)),
        out_shape=jax.ShapeDtypeStruct((DEPTH, nb, 6 * D_MODEL), F32),
        compiler_params=_params(("arbitrary", "arbitrary")),
        name="adaln",
    )(c_all, w_ada, b_ada.reshape(DEPTH, 1, 6 * D_MODEL))


def _modulate_kernel(npt, x_ref, shp, shs, scp, scs, h_ref):
    i = pl.program_id(0)
    shift = _pick(i, npt, shp, shs)
    scale = _pick(i, npt, scp, scs)
    h_ref[...] = (x_ref[...] * (1.0 + scale) + shift).astype(BF16)


def _modulate(x, mp, ms, k_shift, k_scale, tm=256):
    row = pl.BlockSpec((tm, D_MODEL), lambda i: (i, 0))
    return pl.pallas_call(
        functools.partial(_modulate_kernel, T_P // tm),
        grid=(T_ALL // tm,),
        in_specs=[row, *_mod_specs(k_shift, tm), *_mod_specs(k_scale, tm)],
        out_specs=row,
        out_shape=jax.ShapeDtypeStruct((T_ALL, D_MODEL), BF16),
        compiler_params=_params(("parallel",)),
        name="modulate",
    )(x, mp, ms, mp, ms)


def _proj_q_kernel(x_ref, w_ref, q_ref):
    q_ref[...] = (jnp.dot(x_ref[...], w_ref[...], preferred_element_type=F32) * SCALE).astype(BF16)


def _store_kv_rows(ref, r):
    for c in range(ROW_SPLIT):
        ref[pl.ds(c, r.shape[0], stride=ROW_SPLIT), :] = r[:, c * HEAD_DIM:(c + 1) * HEAD_DIM]


def _proj_kv_kernel(npt, x_ref, w_ref, kvp_ref, kvs_ref, kvb_ref):
    i = pl.program_id(0)
    r = jnp.dot(x_ref[...], w_ref[...], preferred_element_type=F32)
    kvb_ref[...] = r.astype(BF16)

    @pl.when(i < npt)
    def _():
        _store_kv_rows(kvp_ref, r)

    @pl.when(i >= npt)
    def _():
        _store_kv_rows(kvs_ref, r)


def _proj_gate_kernel(x_ref, w_ref, g_ref):
    g_ref[...] = jax.nn.sigmoid(jnp.dot(x_ref[...], w_ref[...], preferred_element_type=F32))


def _nsa_project(h, w_in_bf, w_gate_bf, tm=1024, tn=1024):
    nm = T_ALL // tm
    q = pl.pallas_call(
        _proj_q_kernel,
        grid=(nm, N_Q_COLS // tn),
        in_specs=[pl.BlockSpec((tm, D_MODEL), lambda i, j: (i, 0)),
                  pl.BlockSpec((D_MODEL, tn), lambda i, j: (0, j))],
        out_specs=pl.BlockSpec((tm, tn), lambda i, j: (i, j)),
        out_shape=jax.ShapeDtypeStruct((T_ALL, N_Q_COLS), BF16),
        compiler_params=_params(("parallel", "arbitrary")),
        name="nsa_proj_q",
    )(h, w_in_bf)
    npt = T_P // tm
    rows = tm * ROW_SPLIT
    branches = []
    for br in range(3):
        wcol = N_Q_COLS // KV_COLS + br
        branches.append(pl.pallas_call(
            functools.partial(_proj_kv_kernel, npt),
            grid=(nm,),
            in_specs=[pl.BlockSpec((tm, D_MODEL), lambda i: (i, 0)),
                      pl.BlockSpec((D_MODEL, KV_COLS), functools.partial(lambda c, i: (0, c), wcol))],
            out_specs=[pl.BlockSpec((rows, HEAD_DIM), lambda i: (jnp.minimum(i, npt - 1), 0)),
                       pl.BlockSpec((rows, HEAD_DIM), lambda i: (jnp.maximum(i - npt, 0), 0)),
                       pl.BlockSpec((tm, KV_COLS), lambda i: (i, 0))],
            out_shape=[jax.ShapeDtypeStruct((T_P * ROW_SPLIT, HEAD_DIM), F32),
                       jax.ShapeDtypeStruct((T_S * ROW_SPLIT, HEAD_DIM), F32),
                       jax.ShapeDtypeStruct((T_ALL, KV_COLS), BF16)],
            compiler_params=_params(("arbitrary",)),
            name=f"nsa_proj_kv{br}",
        )(h, w_in_bf))
    gates = pl.pallas_call(
        _proj_gate_kernel,
        grid=(nm,),
        in_specs=[pl.BlockSpec((tm, D_MODEL), lambda i: (i, 0)),
                  pl.BlockSpec((D_MODEL, 128), lambda i: (0, 0))],
        out_specs=pl.BlockSpec((tm, 128), lambda i: (i, 0)),
        out_shape=jax.ShapeDtypeStruct((T_ALL, 128), F32),
        compiler_params=_params(("parallel",)),
        name="nsa_proj_gate",
    )(h, w_gate_bf)
    return q, branches, gates


CHUNK_PITCH = CMP_STRIDE * ROW_SPLIT


def _gelu_tanh(x):
    return 0.5 * x * (1.0 + jnp.tanh(0.7978845608028654 * (x + 0.044715 * x * x * x)))


def _compress_core(load, wc_ref, w1_ref, w2_ref, cpos_ref, out_ref):
    n_chunk = PAST_LEN // CMP_STRIDE
    for k2 in range(2):
        cols = []
        for i in range(CMP_STRIDE):
            heads = [load(i, k2 * N_KV_HEADS + h) for h in range(N_KV_HEADS)]
            cols.append(jnp.concatenate(heads, axis=0).astype(BF16))
        x = jnp.concatenate(cols, axis=1)
        part = jnp.dot(x, wc_ref[k2], preferred_element_type=F32)
        nrow = N_KV_HEADS * n_chunk
        hsum = part[:, :HEAD_DIM] + pltpu.roll(part[:, HEAD_DIM:], nrow - 1, axis=0)
        pos_bias = jnp.dot(cpos_ref[k2].astype(BF16), w1_ref[k2], preferred_element_type=F32)[0:1]
        hid = _gelu_tanh(hsum + pos_bias)
        kvc = jnp.dot(hid.astype(BF16), w2_ref[k2], preferred_element_type=F32)
        for h in range(N_KV_HEADS):
            c0 = (k2 * N_KV_HEADS + h) * HEAD_DIM
            out_ref[0, :, c0:c0 + HEAD_DIM] = kvc[h * n_chunk:(h + 1) * n_chunk].astype(BF16)


def _compress_prompt_kernel(x_ref, wc_ref, w1_ref, w2_ref, cpos_ref, out_ref):
    n_chunk = SEQ // CMP_STRIDE

    def load(i, c):
        return x_ref[pl.ds(ROW_SPLIT * i + c, n_chunk, stride=CHUNK_PITCH), :]

    _compress_core(load, wc_ref, w1_ref, w2_ref, cpos_ref, out_ref)


def _compress_sample_kernel(pt_ref, *refs):
    pages = refs[:N_PAGES]
    wc_ref, w1_ref, w2_ref, cpos_ref, out_ref = refs[N_PAGES:]
    chunks_per_page = PAGE_SIZE // CMP_STRIDE

    def load(i, c):
        return jnp.concatenate([pg[pl.ds(ROW_SPLIT * i + c, chunks_per_page, stride=CHUNK_PITCH), :]
                                for pg in pages], axis=0)

    _compress_core(load, wc_ref, w1_ref, w2_ref, cpos_ref, out_ref)


def _compress_weight_specs():
    z3 = lambda *a: (0, 0, 0)
    return [pl.BlockSpec((2, CMP_STRIDE * HEAD_DIM, 2 * HEAD_DIM), z3),
            pl.BlockSpec((2, CMP_BLOCK * HEAD_DIM, HEAD_DIM), z3),
            pl.BlockSpec((2, HEAD_DIM, HEAD_DIM), z3),
            pl.BlockSpec((2, 8, CMP_BLOCK * HEAD_DIM), z3)]


def _compress_prompt(kv_rows, cw):
    return pl.pallas_call(
        _compress_prompt_kernel,
        grid=(BATCH,),
        in_specs=[pl.BlockSpec((SEQ * ROW_SPLIT, HEAD_DIM), lambda b: (b, 0)), *_compress_weight_specs()],
        out_specs=pl.BlockSpec((1, N_CMP, KV_COLS), lambda b: (b, 0, 0)),
        out_shape=jax.ShapeDtypeStruct((BATCH, N_CMP, KV_COLS), BF16),
        compiler_params=_params(("parallel",)),
        name="compress_prompt",
    )(kv_rows, *cw)


def _page_specs():
    return [pl.BlockSpec((PAGE_SIZE * ROW_SPLIT, HEAD_DIM), functools.partial(lambda p, s, pt: (pt[s, p], 0), p))
            for p in range(N_PAGES)]


def _compress_sample(cache_rows, page_table, cw):
    grid_spec = pltpu.PrefetchScalarGridSpec(
        num_scalar_prefetch=1,
        grid=(DEC_BATCH,),
        in_specs=[*_page_specs(), *_compress_weight_specs()],
        out_specs=pl.BlockSpec((1, N_CMP, KV_COLS), lambda s, pt: (s, 0, 0)),
    )
    return pl.pallas_call(
        _compress_sample_kernel,
        grid_spec=grid_spec,
        out_shape=jax.ShapeDtypeStruct((DEC_BATCH, N_CMP, KV_COLS), BF16),
        compiler_params=_params(("arbitrary",)),
        name="compress_sample",
    )(page_table, *([cache_rows] * N_PAGES), *cw)


def _masked_exp(s, mask):
    sm = jnp.where(mask, s, NEG)
    m = jnp.max(sm, axis=-1, keepdims=True)
    e = jnp.where(mask, jnp.exp(sm - m), 0.0)
    return e, jnp.sum(e, axis=-1, keepdims=True)


def _rep4(a):
    return jnp.concatenate([a] * GROUP, axis=0)


def _cmp_and_select(qg, kc, vc, ovt, qpos_col, qpos_row, n_tok):
    nsp = ovt.shape[0]
    s = lax.dot_general(qg, kc, _NT, preferred_element_type=F32)
    n_idx = lax.broadcasted_iota(jnp.int32, (n_tok, N_CMP), 1)
    vis = (n_idx * CMP_STRIDE + (CMP_BLOCK - 1) <= qpos_col) & (n_idx < N_CMP - 1)
    e, l = _masked_exp(s, _rep4(vis))
    p = e / jnp.maximum(l, 1e-30)
    o_c = jnp.dot(p.astype(BF16), vc, preferred_element_type=F32)
    psum = p[0:n_tok] + p[n_tok:2 * n_tok] + p[2 * n_tok:3 * n_tok] + p[3 * n_tok:4 * n_tok]
    if n_tok < 128:
        psum = jnp.concatenate([psum, jnp.zeros((128 - n_tok, N_CMP), F32)], axis=0)
    hi, lo = _split_bf16(psum)
    score = (lax.dot_general(ovt, hi, _NT, preferred_element_type=F32)
             + lax.dot_general(ovt, lo, _NT, preferred_element_type=F32))
    m_idx = lax.broadcasted_iota(jnp.int32, (nsp, 128), 0)
    cur = jnp.right_shift(qpos_row, 6)
    forced = (m_idx == 0) | (m_idx == cur) | (m_idx == cur - 1)
    score = jnp.where(forced, BIG, score)
    score = jnp.where(m_idx > cur, -BIG, score)
    rank = jnp.zeros((nsp, 128), F32)
    for mp in range(nsp):
        row = score[mp:mp + 1, :]
        beats = (row > score) | ((row == score) & (m_idx > mp))
        rank = rank + beats.astype(F32)
    sel_t = ((rank < N_SEL) & (score > -0.5 * BIG)).astype(F32)
    sel_t = jnp.concatenate([sel_t, jnp.zeros((128 - nsp, 128), F32)], axis=0)
    return o_c, sel_t.T.astype(BF16)


def _block_expand(n_keys, key0):
    m_idx = lax.broadcasted_iota(jnp.int32, (128, n_keys), 0)
    c_idx = lax.broadcasted_iota(jnp.int32, (128, n_keys), 1)
    return (m_idx == jnp.right_shift(key0 + c_idx, 6)).astype(BF16)


def _flash(qg, k_ref, v_ref, col, lo, hi, mask_fn):
    rows = qg.shape[0]
    ones = jnp.ones((KEY_TILE, HEAD_DIM), BF16)

    def body(kt, carry):
        m, l, acc = carry
        r0 = pl.multiple_of(kt * KEY_TILE, KEY_TILE)
        k = k_ref[pl.ds(r0, KEY_TILE), col:col + HEAD_DIM]
        v = v_ref[pl.ds(r0, KEY_TILE), N_KV_HEADS * HEAD_DIM + col:N_KV_HEADS * HEAD_DIM + col + HEAD_DIM]
        s = lax.dot_general(qg, k, _NT, preferred_element_type=F32)
        bias = jnp.where(mask_fn(kt * KEY_TILE), 0.0, NEG)
        sm = s + _rep4(bias)
        m_new = jnp.maximum(m, jnp.max(sm, axis=-1, keepdims=True))
        alpha = jnp.exp(m - m_new)
        e = jnp.exp(sm - m_new).astype(BF16)
        pv = jnp.dot(e, jnp.concatenate([v, ones], axis=1), preferred_element_type=F32)
        l = alpha * l + pv[:, HEAD_DIM:HEAD_DIM + 1]
        acc = alpha * acc + pv[:, :HEAD_DIM]
        return m_new, l, acc

    init = (jnp.full((rows, 1), NEG, F32), jnp.zeros((rows, 1), F32), jnp.zeros((rows, HEAD_DIM), F32))
    _, l, acc = lax.fori_loop(lo, hi, body, init)
    return acc / l


def _combine_heads(g, n_tok, o_c, o_s, o_w, gates, o_ref):
    for j in range(GROUP):
        hd = g * GROUP + j
        rows = slice(j * n_tok, (j + 1) * n_tok)
        o = (gates[:, 3 * hd:3 * hd + 1] * o_c[rows] + gates[:, 3 * hd + 1:3 * hd + 2] * o_s[rows]
             + gates[:, 3 * hd + 2:3 * hd + 3] * o_w[rows])
        o_ref[:, hd * HEAD_DIM:(hd + 1) * HEAD_DIM] = o.astype(o_ref.dtype)


def _attn_prompt_kernel(q_ref, ks_ref, kw_ref, kvc_ref, gate_ref, ovt_ref, o_ref):
    b = pl.program_id(0)

    @pl.when(b == BATCH)
    def _():
        o_ref[...] = jnp.zeros_like(o_ref)

    @pl.when(b < BATCH)
    def _():
        _attn_prompt_block(q_ref, ks_ref, kw_ref, kvc_ref, gate_ref, ovt_ref, o_ref)


def _attn_prompt_block(q_ref, ks_ref, kw_ref, kvc_ref, gate_ref, ovt_ref, o_ref):
    qb = pl.program_id(1)
    q0 = qb * Q_BLOCK
    qpos_col = q0 + lax.broadcasted_iota(jnp.int32, (Q_BLOCK, 1), 0)
    qpos_row = q0 + lax.broadcasted_iota(jnp.int32, (1, Q_BLOCK), 1)
    gates = gate_ref[...]
    ovt = ovt_ref[...]
    last_tile = qb // 2 + 1
    c_idx = lax.broadcasted_iota(jnp.int32, (Q_BLOCK, KEY_TILE), 1)
    for g in range(N_KV_HEADS):
        col = g * HEAD_DIM
        qg = jnp.concatenate([q_ref[:, (g * GROUP + j) * HEAD_DIM:(g * GROUP + j + 1) * HEAD_DIM]
                              for j in range(GROUP)], axis=0)
        o_c, sel = _cmp_and_select(qg, kvc_ref[0, :, col:col + HEAD_DIM],
                                   kvc_ref[0, :, N_KV_HEADS * HEAD_DIM + col:N_KV_HEADS * HEAD_DIM + col + HEAD_DIM],
                                   ovt, qpos_col, qpos_row, Q_BLOCK)

        def sel_mask(key0, sel=sel):
            chosen = jnp.dot(sel, _block_expand(KEY_TILE, key0), preferred_element_type=F32) > 0.5
            return chosen & (key0 + c_idx <= qpos_col)

        def win_mask(key0):
            dist = qpos_col - (key0 + c_idx)
            return (dist >= 0) & (dist <= WINDOW)

        o_s = _flash(qg, ks_ref, ks_ref, col, 0, last_tile, sel_mask)
        o_w = _flash(qg, kw_ref, kw_ref, col, jnp.maximum(qb // 2 - 2, 0), last_tile, win_mask)
        _combine_heads(g, Q_BLOCK, o_c, o_s, o_w, gates, o_ref)


def _attn_prompt(q, kvb_sel, kvb_win, kvc, gates, ovt):
    nqb = SEQ // Q_BLOCK
    n_fill = T_S // Q_BLOCK
    last = BATCH - 1
    qrow = lambda b, i: (jnp.minimum(b, last) * nqb + i, 0)
    orow = lambda b, i: (jnp.where(b < BATCH, b * nqb + i, BATCH * nqb + jnp.minimum(i, n_fill - 1)), 0)
    return pl.pallas_call(
        _attn_prompt_kernel,
        grid=(BATCH + 1, nqb),
        in_specs=[pl.BlockSpec((Q_BLOCK, N_Q_COLS), qrow),
                  pl.BlockSpec((SEQ, KV_COLS), lambda b, i: (jnp.minimum(b, last), 0)),
                  pl.BlockSpec((SEQ, KV_COLS), lambda b, i: (jnp.minimum(b, last), 0)),
                  pl.BlockSpec((1, N_CMP, KV_COLS), lambda b, i: (jnp.minimum(b, last), 0, 0)),
                  pl.BlockSpec((Q_BLOCK, 128), qrow),
                  pl.BlockSpec(ovt.shape, lambda b, i: (0, 0))],
        out_specs=pl.BlockSpec((Q_BLOCK, N_Q_COLS), orow),
        out_shape=jax.ShapeDtypeStruct((T_ALL, N_Q_COLS), BF16),
        compiler_params=_params(("arbitrary", "arbitrary")),
        name="attn_prompt",
    )(q, kvb_sel, kvb_win, kvc, gates, ovt)


TAIL = 128
SEL_KEYS = PAST_LEN + TAIL
WIN_KEYS = WINDOW + TAIL
WIN_ROWS = WINDOW * ROW_SPLIT
NEW_ROWS = DEC_SEQ * ROW_SPLIT


def _attn_sample_kernel(pt_ref, *refs):
    pages = refs[:N_PAGES]
    q_ref, news_ref, neww_ref, win_ref, kvc_ref, gate_ref, ovt_ref, o_ref, nwin_ref, sbuf, wbuf = refs[N_PAGES:]
    pad = jnp.zeros((TAIL - DEC_SEQ, HEAD_DIM), F32)
    for c in range(ROW_SPLIT):
        for p in range(N_PAGES):
            sbuf[c, p * PAGE_SIZE:(p + 1) * PAGE_SIZE, :] = (
                pages[p][pl.ds(c, PAGE_SIZE, stride=ROW_SPLIT), :].astype(BF16))
        sbuf[c, PAST_LEN:, :] = jnp.concatenate(
            [news_ref[pl.ds(c, DEC_SEQ, stride=ROW_SPLIT), :], pad], axis=0).astype(BF16)
        wbuf[c, :WINDOW, :] = win_ref[pl.ds(c, WINDOW, stride=ROW_SPLIT), :].astype(BF16)
        wbuf[c, WINDOW:, :] = jnp.concatenate(
            [neww_ref[pl.ds(c, DEC_SEQ, stride=ROW_SPLIT), :], pad], axis=0).astype(BF16)
    nwin_ref[:WIN_ROWS - NEW_ROWS, :] = win_ref[NEW_ROWS:, :]
    nwin_ref[WIN_ROWS - NEW_ROWS:, :] = neww_ref[...]

    tcol = lax.broadcasted_iota(jnp.int32, (DEC_SEQ, 1), 0)
    qpos_col = PAST_LEN + tcol
    qpos_row = PAST_LEN + jnp.minimum(lax.broadcasted_iota(jnp.int32, (1, 128), 1), DEC_SEQ - 1)
    gates = gate_ref[...]
    ovt = ovt_ref[...]
    expand = _block_expand(SEL_KEYS, 0)
    sel_pos = lax.broadcasted_iota(jnp.int32, (DEC_SEQ, SEL_KEYS), 1)
    win_c = lax.broadcasted_iota(jnp.int32, (DEC_SEQ, WIN_KEYS), 1)
    win_mask = _rep4((win_c >= tcol) & (win_c <= WINDOW + tcol))
    for g in range(N_KV_HEADS):
        col = g * HEAD_DIM
        vcol = N_KV_HEADS * HEAD_DIM + col
        qg = jnp.concatenate([q_ref[:, (g * GROUP + j) * HEAD_DIM:(g * GROUP + j + 1) * HEAD_DIM]
                              for j in range(GROUP)], axis=0).astype(BF16)
        o_c, sel = _cmp_and_select(qg, kvc_ref[0, :, col:col + HEAD_DIM], kvc_ref[0, :, vcol:vcol + HEAD_DIM],
                                   ovt, qpos_col, qpos_row, DEC_SEQ)
        chosen = jnp.dot(sel, expand, preferred_element_type=F32)[:DEC_SEQ] > 0.5
        mask = _rep4(chosen & (sel_pos <= qpos_col))
        s = lax.dot_general(qg, sbuf[g], _NT, preferred_element_type=F32)
        e, l = _masked_exp(s, mask)
        o_s = jnp.dot(e.astype(BF16), sbuf[N_KV_HEADS + g], preferred_element_type=F32)
        o_s = o_s / jnp.maximum(l, 1e-30)
        s = lax.dot_general(qg, wbuf[g], _NT, preferred_element_type=F32)
        e, l = _masked_exp(s, win_mask)
        o_w = jnp.dot(e.astype(BF16), wbuf[N_KV_HEADS + g], preferred_element_type=F32)
        o_w = o_w / jnp.maximum(l, 1e-30)
        _combine_heads(g, DEC_SEQ, o_c, o_s, o_w, gates, o_ref)


def _attn_sample(q_s, new_sel, new_win, cache_sel, state_win, page_table, kvc, gates, ovt):
    row0 = T_P // DEC_SEQ
    grid_spec = pltpu.PrefetchScalarGridSpec(
        num_scalar_prefetch=1,
        grid=(DEC_BATCH,),
        in_specs=[*_page_specs(),
                  pl.BlockSpec((DEC_SEQ, N_Q_COLS), lambda s, pt: (s, 0)),
                  pl.BlockSpec((NEW_ROWS, HEAD_DIM), lambda s, pt: (s, 0)),
                  pl.BlockSpec((NEW_ROWS, HEAD_DIM), lambda s, pt: (s, 0)),
                  pl.BlockSpec((WIN_ROWS, HEAD_DIM), lambda s, pt: (s, 0)),
                  pl.BlockSpec((1, N_CMP, KV_COLS), lambda s, pt: (s, 0, 0)),
                  pl.BlockSpec((DEC_SEQ, 128), lambda s, pt: (row0 + s, 0)),
                  pl.BlockSpec(ovt.shape, lambda s, pt: (0, 0))],
        out_specs=[pl.BlockSpec((DEC_SEQ, N_Q_COLS), lambda s, pt: (s, 0)),
                   pl.BlockSpec((WIN_ROWS, HEAD_DIM), lambda s, pt: (s, 0))],
        scratch_shapes=[pltpu.VMEM((ROW_SPLIT, SEL_KEYS, HEAD_DIM), BF16),
                        pltpu.VMEM((ROW_SPLIT, WIN_KEYS, HEAD_DIM), BF16)],
    )
    return pl.pallas_call(
        _attn_sample_kernel,
        grid_spec=grid_spec,
        out_shape=[jax.ShapeDtypeStruct((T_S, N_Q_COLS), F32),
                   jax.ShapeDtypeStruct((DEC_BATCH * WIN_ROWS, HEAD_DIM), F32)],
        compiler_params=_params(("arbitrary",)),
        name="attn_sample",
    )(page_table, *([cache_sel] * N_PAGES), q_s, new_sel, new_win, state_win, kvc, gates, ovt)


def _overlap_t(ns, nsp):
    c0 = jnp.arange(N_CMP)[None, :] * CMP_STRIDE
    s0 = jnp.arange(nsp)[:, None] * SEL_BLOCK
    ov = jnp.minimum(c0 + CMP_BLOCK, s0 + SEL_BLOCK) - jnp.maximum(c0, s0)
    ov = jnp.maximum(ov, 0).astype(F32) / CMP_BLOCK
    ov = jnp.where((jnp.arange(nsp)[:, None] < ns) & (jnp.arange(N_CMP)[None, :] < N_CMP - 1), ov, 0.0)
    return ov.astype(BF16)


def _outproj_kernel(npt, o_ref, w_ref, x_ref, gp, gs, shp, shs, scp, scs, lng, lnb, x1_ref, h1_ref):
    i = pl.program_id(0)
    y = jnp.dot(o_ref[...], w_ref[...], preferred_element_type=F32)
    x1 = _layer_norm(ALPHA * x_ref[...] + _pick(i, npt, gp, gs) * y, lng[...], lnb[...])
    x1_ref[...] = x1
    h1_ref[...] = (x1 * (1.0 + _pick(i, npt, scp, scs)) + _pick(i, npt, shp, shs)).astype(BF16)


def _outproj(o, w_out_bf, x, mp, ms, lng, lnb, tm=256):
    row = pl.BlockSpec((tm, D_MODEL), lambda i: (i, 0))
    vec = pl.BlockSpec((1, D_MODEL), lambda i: (0, 0))
    return pl.pallas_call(
        functools.partial(_outproj_kernel, T_P // tm),
        grid=(T_ALL // tm,),
        in_specs=[row, pl.BlockSpec((N_Q_COLS, D_MODEL), lambda i: (0, 0)), row,
                  *_mod_specs(2, tm), *_mod_specs(3, tm), *_mod_specs(4, tm), vec, vec],
        out_specs=[row, row],
        out_shape=[jax.ShapeDtypeStruct((T_ALL, D_MODEL), F32), jax.ShapeDtypeStruct((T_ALL, D_MODEL), BF16)],
        compiler_params=_params(("parallel",)),
        name="nsa_out_postnorm",
    )(o, w_out_bf, x, mp, ms, mp, ms, mp, ms, lng, lnb)


def _swiglu_up_kernel(x_ref, wg_ref, wu_ref, a_ref):
    x = x_ref[...]
    g = jnp.dot(x, wg_ref[...], preferred_element_type=F32)
    u = jnp.dot(x, wu_ref[...], preferred_element_type=F32)
    a_ref[...] = (g * jax.nn.sigmoid(g) * u).astype(BF16)


def _ffn_up(h, w_up_bf, tm=1024, tn=512):
    nj = D_FF // tn
    return pl.pallas_call(
        _swiglu_up_kernel,
        grid=(T_ALL // tm, nj),
        in_specs=[pl.BlockSpec((tm, D_MODEL), lambda i, j: (i, 0)),
                  pl.BlockSpec((D_MODEL, tn), lambda i, j: (0, j)),
                  pl.BlockSpec((D_MODEL, tn), lambda i, j: (0, nj + j))],
        out_specs=pl.BlockSpec((tm, tn), lambda i, j: (i, j)),
        out_shape=jax.ShapeDtypeStruct((T_ALL, D_FF), BF16),
        compiler_params=_params(("parallel", "arbitrary")),
        name="ffn_up",
    )(h, w_up_bf, w_up_bf)


def _ffn_down_kernel(npt, a_ref, w_ref, x_ref, gp, gs, lng, lnb, x2_ref, acc):
    i = pl.program_id(0)
    k = pl.program_id(1)

    @pl.when(k == 0)
    def _():
        acc[...] = jnp.zeros_like(acc)

    acc[...] += jnp.dot(a_ref[...], w_ref[...], preferred_element_type=F32)

    @pl.when(k == pl.num_programs(1) - 1)
    def _():
        x2_ref[...] = _layer_norm(ALPHA * x_ref[...] + _pick(i, npt, gp, gs) * acc[...], lng[...], lnb[...])


def _ffn_down(a, w_down_bf, x, mp, ms, lng, lnb, tm=512, tk=512):
    row = pl.BlockSpec((tm, D_MODEL), lambda i, k: (i, 0))
    vec = pl.BlockSpec((1, D_MODEL), lambda i, k: (0, 0))
    return pl.pallas_call(
        functools.partial(_ffn_down_kernel, T_P // tm),
        grid=(T_ALL // tm, D_FF // tk),
        in_specs=[pl.BlockSpec((tm, tk), lambda i, k: (i, k)),
                  pl.BlockSpec((tk, D_MODEL), lambda i, k: (k, 0)),
                  row, *_mod_specs(5, tm), vec, vec],
        out_specs=row,
        out_shape=jax.ShapeDtypeStruct((T_ALL, D_MODEL), F32),
        scratch_shapes=[pltpu.VMEM((tm, D_MODEL), F32)],
        compiler_params=_params(("parallel", "arbitrary")),
        name="ffn_down_postnorm",
    )(a, w_down_bf, x, mp, ms, lng, lnb)


HALO = 16


def _pool_kernel(prompt, tiles_per_seq, n_tiles, *refs):
    i = pl.program_id(0)
    x1_ref, h1_ref, _, lg_ref = refs[-4:]

    @pl.when(i >= n_tiles)
    def _():
        x1_ref[...] = jnp.zeros_like(x1_ref)
        h1_ref[...] = jnp.zeros_like(h1_ref)
        lg_ref[...] = jnp.zeros_like(lg_ref)

    @pl.when(i < n_tiles)
    def _():
        _pool_tile(prompt, tiles_per_seq, *refs)


def _pool_tile(prompt, tiles_per_seq, x_ref, halo_ref, sh_m, sc_m, g_m, sh_f, sc_f, hsh, hsc,
               wp_ref, ps_ref, lng, lnb, wr_hi, wr_lo, br_ref, x1_ref, h1_ref, hm_ref, lg_ref):
    i = pl.program_id(0)
    tm = x_ref.shape[0]
    x = x_ref[...]
    h = x * (1.0 + sc_m[...]) + sh_m[...]
    hm_ref[...] = h
    if prompt:
        hh = halo_ref[...] * (1.0 + hsc[...]) + hsh[...]
        hh = jnp.where(i % tiles_per_seq == 0, 0.0, hh)
        pos = (i % tiles_per_seq) * tm + lax.broadcasted_iota(jnp.int32, (tm, 1), 0)
    else:
        hh = jnp.zeros((HALO, D_MODEL), F32)
    ext = jnp.concatenate([hh, h], axis=0)
    ys = []
    for gi, w in enumerate(POOL_WINDOWS):
        a = ext[:, gi * POOL_GROUP:(gi + 1) * POOL_GROUP]
        s = a
        d = 1
        while d < w:
            s = s + pltpu.roll(s, d, axis=0)
            d *= 2
        s = s[HALO:]
        if prompt:
            inv = 1.0 / jnp.minimum(pos + 1, w).astype(F32)
        else:
            inv = 1.0 / w
        diff = (s * inv - a[HALO:]).astype(BF16)
        ys.append(jnp.dot(diff, wp_ref[gi], preferred_element_type=F32))
    y = jnp.concatenate(ys, axis=1) * ps_ref[...]
    x1 = _layer_norm(ALPHA * x + g_m[...] * y, lng[...], lnb[...])
    x1_ref[...] = x1
    h1 = x1 * (1.0 + sc_f[...]) + sh_f[...]
    h1_ref[...] = h1
    hi, lo = _split_bf16(h1)
    lg_ref[...] = (jnp.dot(hi, wr_hi[...], preferred_element_type=F32)
                   + jnp.dot(hi, wr_lo[...], preferred_element_type=F32)
                   + jnp.dot(lo, wr_hi[...], preferred_element_type=F32) + br_ref[...])


def _pool_call(prompt, x, n, n_out, hm_spec, hm_rows, mods, w_pool_bf, pool_scale, lng, lnb, wr_hi, wr_lo,
               b_router, tm):
    row = pl.BlockSpec((tm, D_MODEL), lambda i: (i, 0))
    vec = pl.BlockSpec((1, D_MODEL), lambda i: (0, 0))
    halo = pl.BlockSpec((HALO, D_MODEL), lambda i: (jnp.maximum(i * (tm // HALO) - 1, 0), 0))
    specs = [s for s, _ in mods]
    arrs = [a for _, a in mods]
    return pl.pallas_call(
        functools.partial(_pool_kernel, prompt, SEQ // tm, n // tm),
        grid=(n_out // tm,),
        in_specs=[row, halo, *specs,
                  pl.BlockSpec((4, POOL_GROUP, POOL_GROUP), lambda i: (0, 0, 0)), vec, vec, vec,
                  pl.BlockSpec((D_MODEL, 128), lambda i: (0, 0)), pl.BlockSpec((D_MODEL, 128), lambda i: (0, 0)),
                  pl.BlockSpec((1, 128), lambda i: (0, 0))],
        out_specs=[row, row, hm_spec, pl.BlockSpec((tm, 128), lambda i: (i, 0))],
        out_shape=[jax.ShapeDtypeStruct((n_out, D_MODEL), F32), jax.ShapeDtypeStruct((n_out, D_MODEL), F32),
                   jax.ShapeDtypeStruct((hm_rows, D_MODEL), F32), jax.ShapeDtypeStruct((n_out, 128), F32)],
        compiler_params=_params(("arbitrary",)),
        name="pool_prompt" if prompt else "pool_sample",
    )(x, x, *arrs, w_pool_bf, pool_scale, lng, lnb, wr_hi, wr_lo, b_router)


_POOL_MOD_CHUNKS = (0, 1, 2, 3, 4, 0, 1)


def _pool_prompt(x, mp, *rest, tm=256):
    tpb = SEQ // tm
    seq = lambda i: jnp.minimum(i // tpb, BATCH - 1)
    mods = [(pl.BlockSpec((None, 1, D_MODEL), functools.partial(lambda k, i: (seq(i), 0, k), k)), mp)
            for k in _POOL_MOD_CHUNKS]
    hm_spec = pl.BlockSpec((tm, D_MODEL), lambda i: (seq(i), 0))
    return _pool_call(True, x, T_P, T_ALL, hm_spec, BATCH * tm, mods, *rest, tm)


def _pool_sample(x_ext, ms_ext, *rest, tm=8 * POOL_EXT):
    n = DEC_BATCH * POOL_EXT
    mods = [(pl.BlockSpec((tm, D_MODEL), functools.partial(lambda k, i: (i, k), k)), ms_ext)
            for k in _POOL_MOD_CHUNKS[:5]]
    mods += [(pl.BlockSpec((HALO, D_MODEL), functools.partial(lambda k, i: (0, k), k)), ms_ext)
             for k in _POOL_MOD_CHUNKS[5:]]
    hm_spec = pl.BlockSpec((tm, D_MODEL), lambda i: (i, 0))
    return _pool_call(False, x_ext, n, n, hm_spec, n, mods, *rest, tm)


N_MOE_BLOCKS = -(-(T_ALL * TOP_K) // MOE_BLOCK) + N_EXPERTS
N_SLOTS = N_MOE_BLOCKS * MOE_BLOCK


def _row_gather_kernel(idx_ref, src_hbm, o_ref, buf, sem):
    i = pl.program_id(0)
    nb = pl.num_programs(0)
    bsz = o_ref.shape[0]

    def issue(blk, slot):
        def body(r, carry):
            row = idx_ref[blk * bsz + r]
            pltpu.make_async_copy(src_hbm.at[pl.ds(row, 1)], buf.at[slot, pl.ds(r, 1)], sem.at[slot]).start()
            return carry
        lax.fori_loop(0, bsz, body, 0, unroll=8)

    @pl.when(i == 0)
    def _():
        issue(0, 0)

    @pl.when(i + 1 < nb)
    def _():
        issue(i + 1, (i + 1) % 2)

    slot = i % 2
    pltpu.make_async_copy(src_hbm.at[pl.ds(0, bsz)], buf.at[slot], sem.at[slot]).wait()
    o_ref[...] = buf[slot].astype(BF16)


def _gather_rows_bf16(src, idx):
    grid_spec = pltpu.PrefetchScalarGridSpec(
        num_scalar_prefetch=1,
        grid=(N_MOE_BLOCKS,),
        in_specs=[pl.BlockSpec(memory_space=pl.ANY)],
        out_specs=pl.BlockSpec((MOE_BLOCK, D_MODEL), lambda i, idx: (i, 0)),
        scratch_shapes=[pltpu.VMEM((2, MOE_BLOCK, D_MODEL), F32), pltpu.SemaphoreType.DMA((2,))],
    )
    return pl.pallas_call(
        _row_gather_kernel,
        grid_spec=grid_spec,
        out_shape=jax.ShapeDtypeStruct((N_SLOTS, D_MODEL), BF16),
        compiler_params=_params(("arbitrary",)),
        name="moe_dispatch_gather",
    )(idx, src)


def _moe_up_kernel(be_ref, fresh_ref, live_ref, x_ref, wg_ref, wu_ref, a_ref, wg_s, wu_s):
    i = pl.program_id(1)

    @pl.when(fresh_ref[i] == 1)
    def _():
        wg_s[...] = wg_ref[...].astype(BF16)
        wu_s[...] = wu_ref[...].astype(BF16)

    @pl.when(live_ref[i] == 1)
    def _():
        x = x_ref[...]
        g = jnp.dot(x, wg_s[...], preferred_element_type=F32)
        u = jnp.dot(x, wu_s[...], preferred_element_type=F32)
        a_ref[...] = (g * jax.nn.sigmoid(g) * u).astype(BF16)

    @pl.when(live_ref[i] == 0)
    def _():
        a_ref[...] = jnp.zeros_like(a_ref)


def _moe_up(xb, w_up, block_expert, fresh, live, tn=512):
    nj = D_FF_EXPERT // tn
    grid_spec = pltpu.PrefetchScalarGridSpec(
        num_scalar_prefetch=3,
        grid=(nj, N_MOE_BLOCKS),
        in_specs=[pl.BlockSpec((MOE_BLOCK, D_MODEL), lambda j, i, be, fr, lv: (i, 0)),
                  pl.BlockSpec((None, D_MODEL, tn), lambda j, i, be, fr, lv: (be[i], 0, j)),
                  pl.BlockSpec((None, D_MODEL, tn), lambda j, i, be, fr, lv: (be[i], 0, nj + j))],
        out_specs=pl.BlockSpec((MOE_BLOCK, tn), lambda j, i, be, fr, lv: (i, j)),
        scratch_shapes=[pltpu.VMEM((D_MODEL, tn), BF16), pltpu.VMEM((D_MODEL, tn), BF16)],
    )
    return pl.pallas_call(
        _moe_up_kernel,
        grid_spec=grid_spec,
        out_shape=jax.ShapeDtypeStruct((N_SLOTS, D_FF_EXPERT), BF16),
        compiler_params=_params(("arbitrary", "arbitrary")),
        name="moe_up",
    )(block_expert, fresh, live, xb, w_up, w_up)


def _moe_down_kernel(be_ref, fresh_ref, live_ref, a_ref, w_ref, sg_ref, y_ref, w_s):
    i = pl.program_id(1)

    @pl.when(fresh_ref[i] == 1)
    def _():
        w_s[...] = w_ref[...].astype(BF16)

    @pl.when(live_ref[i] == 1)
    def _():
        y_ref[...] = jnp.dot(a_ref[...], w_s[...], preferred_element_type=F32) * sg_ref[...]

    @pl.when(live_ref[i] == 0)
    def _():
        y_ref[...] = jnp.zeros_like(y_ref)


def _moe_down(act, w_down, slot_gate, block_expert, fresh, live, tn=512):
    grid_spec = pltpu.PrefetchScalarGridSpec(
        num_scalar_prefetch=3,
        grid=(D_MODEL // tn, N_MOE_BLOCKS),
        in_specs=[pl.BlockSpec((MOE_BLOCK, D_FF_EXPERT), lambda j, i, be, fr, lv: (i, 0)),
                  pl.BlockSpec((None, D_FF_EXPERT, tn), lambda j, i, be, fr, lv: (be[i], 0, j)),
                  pl.BlockSpec((MOE_BLOCK, 1), lambda j, i, be, fr, lv: (i, 0))],
        out_specs=pl.BlockSpec((MOE_BLOCK, tn), lambda j, i, be, fr, lv: (i, j)),
        scratch_shapes=[pltpu.VMEM((D_FF_EXPERT, tn), BF16)],
    )
    return pl.pallas_call(
        _moe_down_kernel,
        grid_spec=grid_spec,
        out_shape=jax.ShapeDtypeStruct((N_SLOTS, D_MODEL), F32),
        compiler_params=_params(("arbitrary", "arbitrary")),
        name="moe_down",
    )(block_expert, fresh, live, act, w_down, slot_gate)


def _route(logits):
    n = logits.shape[0]
    top_logit, top_e = lax.top_k(logits, TOP_K)
    gate = jax.nn.softmax(top_logit, axis=-1)
    e_flat = top_e.reshape(-1)
    order = jnp.argsort(e_flat, stable=True)
    e_sorted = e_flat[order]
    counts = jnp.bincount(e_flat, length=N_EXPERTS)
    padded = (counts + MOE_BLOCK - 1) // MOE_BLOCK * MOE_BLOCK
    start = jnp.cumsum(counts) - counts
    pend = jnp.cumsum(padded)
    pstart = pend - padded
    dest = (pstart[e_sorted] + jnp.arange(n * TOP_K) - start[e_sorted]).astype(jnp.int32)
    blk0 = jnp.arange(N_MOE_BLOCKS) * MOE_BLOCK
    block_expert = jnp.minimum(jnp.searchsorted(pend, blk0, side='right'), N_EXPERTS - 1).astype(jnp.int32)
    live = (blk0 < pend[-1]).astype(jnp.int32)
    fresh = jnp.concatenate([jnp.ones((1,), jnp.int32),
                             (block_expert[1:] != block_expert[:-1]).astype(jnp.int32)])
    slot = jnp.arange(N_SLOTS)
    se = jnp.repeat(block_expert, MOE_BLOCK)
    off = slot - pstart[se]
    valid = off < counts[se]
    src = jnp.clip(start[se] + off, 0, n * TOP_K - 1)
    flat = order[src]
    slot_tok = jnp.where(valid, flat // TOP_K, n - 1).astype(jnp.int32)
    slot_gate = jnp.where(valid, gate.reshape(-1)[flat], 0.0)
    slot_of = dest[jnp.argsort(order)].reshape(n, TOP_K)
    return slot_tok, slot_gate, block_expert, fresh, live, slot_of


def _final_kernel(npt, sa_ref, sb_ref, x_ref, y_hbm, gp, gs, lng, lnb, op_ref, os_ref, buf, sem):
    i = pl.program_id(0)
    nb = pl.num_programs(0)
    tm = x_ref.shape[0]

    def issue(blk, slot):
        def body(r, carry):
            t = blk * tm + r
            pltpu.make_async_copy(y_hbm.at[pl.ds(sa_ref[t], 1)], buf.at[slot, 0, pl.ds(r, 1)], sem.at[slot]).start()
            pltpu.make_async_copy(y_hbm.at[pl.ds(sb_ref[t], 1)], buf.at[slot, 1, pl.ds(r, 1)], sem.at[slot]).start()
            return carry
        lax.fori_loop(0, tm, body, 0, unroll=8)

    @pl.when(i == 0)
    def _():
        issue(0, 0)

    @pl.when(i + 1 < nb)
    def _():
        issue(i + 1, (i + 1) % 2)

    slot = i % 2
    for half in range(TOP_K):
        pltpu.make_async_copy(y_hbm.at[pl.ds(0, tm)], buf.at[slot, half], sem.at[slot]).wait()
    y = buf[slot, 0] + buf[slot, 1]
    out = _layer_norm(ALPHA * x_ref[...] + _pick(i, npt, gp, gs) * y, lng[...], lnb[...])

    @pl.when(i < npt)
    def _():
        op_ref[...] = out

    @pl.when(i >= npt)
    def _():
        os_ref[...] = out


def _final(x, yb, slot_a, slot_b, mp, ms, lng, lnb, tm=256):
    npt = T_P // tm
    row = pl.BlockSpec((tm, D_MODEL), lambda i, *_: (i, 0))
    vec = pl.BlockSpec((1, D_MODEL), lambda i, *_: (0, 0))
    grid_spec = pltpu.PrefetchScalarGridSpec(
        num_scalar_prefetch=2,
        grid=(T_ALL // tm,),
        in_specs=[row, pl.BlockSpec(memory_space=pl.ANY), *_mod_specs(5, tm), vec, vec],
        out_specs=[pl.BlockSpec((tm, D_MODEL), lambda i, *_: (jnp.minimum(i, npt - 1), 0)),
                   pl.BlockSpec((tm, D_MODEL), lambda i, *_: (jnp.maximum(i - npt, 0), 0))],
        scratch_shapes=[pltpu.VMEM((2, 2, tm, D_MODEL), F32), pltpu.SemaphoreType.DMA((2,))],
    )
    return pl.pallas_call(
        functools.partial(_final_kernel, npt),
        grid_spec=grid_spec,
        out_shape=[jax.ShapeDtypeStruct((T_P, D_MODEL), F32), jax.ShapeDtypeStruct((T_S, D_MODEL), F32)],
        compiler_params=_params(("arbitrary",)),
        name="moe_combine_postnorm",
    )(slot_a, slot_b, x, yb, mp, ms, lng, lnb)


def kernel(x_prompt, x_sample, cache_cmp_kv, cache_sel_kv, state_win_kv, state_pool, page_table,
           c_prompt, c_sample, w_ada, b_ada, ln_gain, ln_bias, w_nsa_in, w_cmp1, w_cmp2, cmp_pos,
           w_nsa_out, w_pool, pool_scale, w_ffn_up, w_ffn_down, w_router, b_router, w_moe_up, w_moe_down):
    x = jnp.concatenate([x_prompt.reshape(T_P, D_MODEL), x_sample.reshape(T_S, D_MODEL)], axis=0)
    mods = _ada(jnp.concatenate([c_prompt, c_sample], axis=0), w_ada, b_ada)

    def layer_mods(i):
        return mods[i, :BATCH].reshape(BATCH, 1, 6 * D_MODEL), jnp.repeat(mods[i, BATCH:], DEC_SEQ, axis=0)

    mp, ms = layer_mods(0)
    lng = ln_gain[0].reshape(2, 1, D_MODEL)
    lnb = ln_bias[0].reshape(2, 1, D_MODEL)
    h = _modulate(x, mp, ms, 0, 1)
    w_in_bf = w_nsa_in[0].astype(BF16)
    w_gate_bf = jnp.pad(w_in_bf[:, N_Q_COLS + 3 * KV_COLS:], ((0, 0), (0, 128 - N_GATE_COLS)))
    q, (cmp_kv, sel_kv, win_kv), gates = _nsa_project(h, w_in_bf, w_gate_bf)

    w1 = w_cmp1[0].astype(BF16)
    wc = w1.reshape(2, 2, CMP_STRIDE, HEAD_DIM, HEAD_DIM).transpose(0, 2, 3, 1, 4).reshape(
        2, CMP_STRIDE * HEAD_DIM, 2 * HEAD_DIM)
    cpos = jnp.pad(cmp_pos[0].reshape(2, 1, CMP_BLOCK * HEAD_DIM), ((0, 0), (0, 7), (0, 0)))
    cw = (wc, w1, w_cmp2[0].astype(BF16), cpos)
    stored = lambda a: a.reshape(-1, HEAD_DIM)
    kvc_p = _compress_prompt(cmp_kv[0], cw)
    kvc_s = _compress_sample(stored(cache_cmp_kv), page_table, cw)

    o = _attn_prompt(q, sel_kv[2], win_kv[2], kvc_p, gates, _overlap_t(SEQ // SEL_BLOCK, 32))
    ns_s = -(-(PAST_LEN + DEC_SEQ) // SEL_BLOCK)
    o_s, new_win_rows = _attn_sample(q[T_P:].astype(F32), sel_kv[1], win_kv[1], stored(cache_sel_kv),
                                     stored(state_win_kv), page_table, kvc_s, gates, _overlap_t(ns_s, 40))
    o = lax.dynamic_update_slice(o, o_s.astype(BF16), (T_P, 0))
    x1, h1 = _outproj(o, w_nsa_out[0].astype(BF16), x, mp, ms, lng[0], lnb[0])
    act = _ffn_up(h1, w_ffn_up[0].astype(BF16))
    x2 = _ffn_down(act, w_ffn_down[0].astype(BF16), x1, mp, ms, lng[1], lnb[1])

    kv_p = lambda a: a.reshape(1, BATCH, SEQ, 2, N_KV_HEADS, HEAD_DIM)
    kv_s = lambda a: a.reshape(1, DEC_BATCH, -1, 2, N_KV_HEADS, HEAD_DIM)
    new_cmp_p, new_cmp_s = kv_p(cmp_kv[0]), kv_s(cmp_kv[1])
    new_sel_p, new_sel_s = kv_p(sel_kv[0]), kv_s(sel_kv[1])
    new_win_p = kv_p(win_kv[0])[:, :, SEQ - WINDOW:]
    new_win_s = kv_s(new_win_rows)

    mp, ms = layer_mods(1)
    lng = ln_gain[1].reshape(2, 1, D_MODEL)
    lnb = ln_bias[1].reshape(2, 1, D_MODEL)
    wr_pad = jnp.pad(w_router[0], ((0, 0), (0, 128 - N_EXPERTS)))
    wr_hi, wr_lo = _split_bf16(wr_pad)
    br = jnp.pad(b_router[0].reshape(1, N_EXPERTS), ((0, 0), (0, 128 - N_EXPERTS)))
    pool_args = (w_pool[0].astype(BF16), pool_scale[0].reshape(1, D_MODEL), lng[0], lnb[0], wr_hi, wr_lo, br)
    x3, h3, hmp, lg = _pool_prompt(x2, mp, *pool_args)

    ms3 = ms.reshape(DEC_BATCH, DEC_SEQ, 6 * D_MODEL)
    ms_ext = jnp.pad(ms3, ((0, 0), (POOL_EXT - DEC_SEQ, 0), (0, 0))).reshape(DEC_BATCH * POOL_EXT, 6 * D_MODEL)
    xs_ext = jnp.concatenate([jnp.zeros((DEC_BATCH, 1, D_MODEL), F32), state_pool[0],
                              x2[T_P:].reshape(DEC_BATCH, DEC_SEQ, D_MODEL)], axis=1)
    xs_ext = xs_ext.reshape(DEC_BATCH * POOL_EXT, D_MODEL)
    x1s, h1s, hms, lgs = _pool_sample(xs_ext, ms_ext, *pool_args)
    new_rows = lambda a: a.reshape(DEC_BATCH, POOL_EXT, -1)[:, POOL_EXT - DEC_SEQ:].reshape(T_S, -1)
    x3 = lax.dynamic_update_slice(x3, new_rows(x1s), (T_P, 0))
    h3 = lax.dynamic_update_slice(h3, new_rows(h1s), (T_P, 0))
    logits = lax.dynamic_update_slice(lg, new_rows(lgs), (T_P, 0))[:, :N_EXPERTS]
    new_pool_p = hmp.reshape(BATCH, -1, D_MODEL)[None, :, -POOL_BUF:]
    new_pool_s = hms.reshape(DEC_BATCH, POOL_EXT, D_MODEL)[None, :, POOL_EXT - POOL_BUF:]

    slot_tok, slot_gate, block_expert, fresh, live, slot_of = _route(logits)
    xb = _gather_rows_bf16(h3, slot_tok)
    act = _moe_up(xb, w_moe_up[0], block_expert, fresh, live)
    yb = _moe_down(act, w_moe_down[0], slot_gate.reshape(-1, 1), block_expert, fresh, live)
    y_prompt, y_sample = _final(x3, yb, slot_of[:, 0], slot_of[:, 1], mp, ms, lng[1], lnb[1])

    return (y_prompt.reshape(BATCH, SEQ, D_MODEL), y_sample.reshape(DEC_BATCH, DEC_SEQ, D_MODEL),
            new_cmp_p, new_cmp_s, new_sel_p, new_sel_s, new_win_p, new_win_s, new_pool_p, new_pool_s)
```

```python
import functools

import jax
import jax.numpy as jnp
from jax import lax
from jax.experimental import pallas as pl
from jax.experimental.pallas import tpu as pltpu

F32 = jnp.float32
BF16 = jnp.bfloat16

D_MODEL = 2048
BATCH = 8
SEQ = 2048
DEPTH = 2
DEC_BATCH = 128
DEC_SEQ = 8
PAST_LEN = 2048
PAGE_SIZE = 128
N_PAGES = PAST_LEN // PAGE_SIZE
N_HEADS = 16
HEAD_DIM = 128
N_KV_HEADS = 4
GROUP = 4
CMP_BLOCK = 32
CMP_STRIDE = 16
SEL_BLOCK = 64
N_SEL = 16
WINDOW = 512
Q_BLOCK = 128
SCALE = HEAD_DIM ** -0.5
KV_COLS = 2 * N_KV_HEADS * HEAD_DIM
ROW_SPLIT = 2 * N_KV_HEADS
N_Q_COLS = N_HEADS * HEAD_DIM
N_GATE_COLS = N_HEADS * 3
POOL_WINDOWS = (2, 4, 8, 16)
POOL_GROUP = D_MODEL // 4
POOL_BUF = 15
POOL_EXT = 24
D_FF = 11 * D_MODEL // 4
N_EXPERTS = 8
TOP_K = 2
D_FF_EXPERT = 7 * D_MODEL // 2
MOE_BLOCK = 256
ALPHA = (2.0 * DEPTH) ** 0.25
LN_EPS = 1e-5
BIG = 1e9
NEG = -1e30

T_P = BATCH * SEQ
T_S = DEC_BATCH * DEC_SEQ
T_ALL = T_P + T_S
N_CMP = 128
KEY_TILE = 256
FLASH_GROUPS = 2
VMEM_LIMIT = 56 * 1024 * 1024

_NT = (((1,), (1,)), ((), ()))


def _params(sem, vmem=VMEM_LIMIT):
    return pltpu.CompilerParams(dimension_semantics=sem, vmem_limit_bytes=vmem)


def _layer_norm(z, g, b):
    mu = jnp.mean(z, axis=-1, keepdims=True)
    zc = z - mu
    var = jnp.mean(zc * zc, axis=-1, keepdims=True)
    return zc * lax.rsqrt(var + LN_EPS) * g + b


def _pick(i, n_prompt_tiles, p_ref, s_ref):
    return jnp.where(i < n_prompt_tiles, p_ref[...], s_ref[...])


def _mod_specs(k, tm):
    tpb = SEQ // tm
    npt = T_P // tm
    return (pl.BlockSpec((None, 1, D_MODEL), lambda i, *_: (jnp.minimum(i // tpb, BATCH - 1), 0, k)),
            pl.BlockSpec((tm, D_MODEL), lambda i, *_: (jnp.maximum(i - npt, 0), k)))


def _split_bf16(x):
    hi = x.astype(BF16)
    lo = (x - hi.astype(F32)).astype(BF16)
    return hi, lo


def _ada_kernel(c_ref, w_ref, b_ref, o_ref):
    c = c_ref[...]
    a = (c * jax.nn.sigmoid(c)).astype(BF16)
    o_ref[0] = jnp.dot(a, w_ref[0].astype(BF16), preferred_element_type=F32) + b_ref[0]


def _ada(c_all, w_ada, b_ada):
    nb = c_all.shape[0]
    tn = 1024
    return pl.pallas_call(
        _ada_kernel,
        grid=(DEPTH, 6 * D_MODEL // tn),
        in_specs=[pl.BlockSpec((nb, D_MODEL), lambda l, j: (0, 0)),
                  pl.BlockSpec((1, D_MODEL, tn), lambda l, j: (l, 0, j)),
                  pl.BlockSpec((1, 1, tn), lambda l, j: (l, 0, j))],
        out_specs=pl.BlockSpec((1, nb, tn), lambda l, j: (l, 0, j)),
        out_shape=jax.ShapeDtypeStruct((DEPTH, nb, 6 * D_MODEL), F32),
        compiler_params=_params(("arbitrary", "arbitrary")),
        name="adaln",
    )(c_all, w_ada, b_ada.reshape(DEPTH, 1, 6 * D_MODEL))


def _modulate_kernel(npt, x_ref, shp, shs, scp, scs, h_ref):
    i = pl.program_id(0)
    shift = _pick(i, npt, shp, shs)
    scale = _pick(i, npt, scp, scs)
    h_ref[...] = (x_ref[...] * (1.0 + scale) + shift).astype(BF16)


def _modulate(x, mp, ms, k_shift, k_scale, tm=256):
    row = pl.BlockSpec((tm, D_MODEL), lambda i: (i, 0))
    return pl.pallas_call(
        functools.partial(_modulate_kernel, T_P // tm),
        grid=(T_ALL // tm,),
        in_specs=[row, *_mod_specs(k_shift, tm), *_mod_specs(k_scale, tm)],
        out_specs=row,
        out_shape=jax.ShapeDtypeStruct((T_ALL, D_MODEL), BF16),
        compiler_params=_params(("parallel",)),
        name="modulate",
    )(x, mp, ms, mp, ms)


def _proj_q_kernel(x_ref, w_ref, q_ref):
    q_ref[...] = (jnp.dot(x_ref[...], w_ref[...], preferred_element_type=F32) * SCALE).astype(BF16)


def _store_kv_rows(ref, r):
    for c in range(ROW_SPLIT):
        ref[pl.ds(c, r.shape[0], stride=ROW_SPLIT), :] = r[:, c * HEAD_DIM:(c + 1) * HEAD_DIM]


def _proj_kv_kernel(npt, x_ref, w_ref, kvp_ref, kvs_ref, kvb_ref):
    i = pl.program_id(0)
    r = jnp.dot(x_ref[...], w_ref[...], preferred_element_type=F32)
    kvb_ref[...] = r.astype(BF16)

    @pl.when(i < npt)
    def _():
        _store_kv_rows(kvp_ref, r)

    @pl.when(i >= npt)
    def _():
        _store_kv_rows(kvs_ref, r)


def _proj_gate_kernel(x_ref, w_ref, g_ref):
    g_ref[...] = jax.nn.sigmoid(jnp.dot(x_ref[...], w_ref[...], preferred_element_type=F32))


def _nsa_project(h, w_in_bf, w_gate_bf, tm=1024, tn=1024):
    nm = T_ALL // tm
    q = pl.pallas_call(
        _proj_q_kernel,
        grid=(nm, N_Q_COLS // tn),
        in_specs=[pl.BlockSpec((tm, D_MODEL), lambda i, j: (i, 0)),
                  pl.BlockSpec((D_MODEL, tn), lambda i, j: (0, j))],
        out_specs=pl.BlockSpec((tm, tn), lambda i, j: (i, j)),
        out_shape=jax.ShapeDtypeStruct((T_ALL, N_Q_COLS), BF16),
        compiler_params=_params(("parallel", "arbitrary")),
        name="nsa_proj_q",
    )(h, w_in_bf)
    npt = T_P // tm
    rows = tm * ROW_SPLIT
    branches = []
    for br in range(3):
        wcol = N_Q_COLS // KV_COLS + br
        branches.append(pl.pallas_call(
            functools.partial(_proj_kv_kernel, npt),
            grid=(nm,),
            in_specs=[pl.BlockSpec((tm, D_MODEL), lambda i: (i, 0)),
                      pl.BlockSpec((D_MODEL, KV_COLS), functools.partial(lambda c, i: (0, c), wcol))],
            out_specs=[pl.BlockSpec((rows, HEAD_DIM), lambda i: (jnp.minimum(i, npt - 1), 0)),
                       pl.BlockSpec((rows, HEAD_DIM), lambda i: (jnp.maximum(i - npt, 0), 0)),
                       pl.BlockSpec((tm, KV_COLS), lambda i: (i, 0))],
            out_shape=[jax.ShapeDtypeStruct((T_P * ROW_SPLIT, HEAD_DIM), F32),
                       jax.ShapeDtypeStruct((T_S * ROW_SPLIT, HEAD_DIM), F32),
                       jax.ShapeDtypeStruct((T_ALL, KV_COLS), BF16)],
            compiler_params=_params(("arbitrary",)),
            name=f"nsa_proj_kv{br}",
        )(h, w_in_bf))
    gates = pl.pallas_call(
        _proj_gate_kernel,
        grid=(nm,),
        in_specs=[pl.BlockSpec((tm, D_MODEL), lambda i: (i, 0)),
                  pl.BlockSpec((D_MODEL, 128), lambda i: (0, 0))],
        out_specs=pl.BlockSpec((tm, 128), lambda i: (i, 0)),
        out_shape=jax.ShapeDtypeStruct((T_ALL, 128), F32),
        compiler_params=_params(("parallel",)),
        name="nsa_proj_gate",
    )(h, w_gate_bf)
    return q, branches, gates


CHUNK_PITCH = CMP_STRIDE * ROW_SPLIT


def _gelu_tanh(x):
    return 0.5 * x * (1.0 + jnp.tanh(0.7978845608028654 * (x + 0.044715 * x * x * x)))


def _compress_core(load, wc_ref, w1_ref, w2_ref, cpos_ref, out_ref):
    n_chunk = PAST_LEN // CMP_STRIDE
    for k2 in range(2):
        cols = []
        for i in range(CMP_STRIDE):
            heads = [load(i, k2 * N_KV_HEADS + h) for h in range(N_KV_HEADS)]
            cols.append(jnp.concatenate(heads, axis=0).astype(BF16))
        x = jnp.concatenate(cols, axis=1)
        part = jnp.dot(x, wc_ref[k2], preferred_element_type=F32)
        nrow = N_KV_HEADS * n_chunk
        hsum = part[:, :HEAD_DIM] + pltpu.roll(part[:, HEAD_DIM:], nrow - 1, axis=0)
        pos_bias = jnp.dot(cpos_ref[k2].astype(BF16), w1_ref[k2], preferred_element_type=F32)[0:1]
        hid = _gelu_tanh(hsum + pos_bias)
        kvc = jnp.dot(hid.astype(BF16), w2_ref[k2], preferred_element_type=F32)
        for h in range(N_KV_HEADS):
            c0 = (k2 * N_KV_HEADS + h) * HEAD_DIM
            out_ref[0, :, c0:c0 + HEAD_DIM] = kvc[h * n_chunk:(h + 1) * n_chunk].astype(BF16)


def _compress_prompt_kernel(x_ref, wc_ref, w1_ref, w2_ref, cpos_ref, out_ref):
    n_chunk = SEQ // CMP_STRIDE

    def load(i, c):
        return x_ref[pl.ds(ROW_SPLIT * i + c, n_chunk, stride=CHUNK_PITCH), :]

    _compress_core(load, wc_ref, w1_ref, w2_ref, cpos_ref, out_ref)


def _compress_sample_kernel(pt_ref, *refs):
    pages = refs[:N_PAGES]
    wc_ref, w1_ref, w2_ref, cpos_ref, out_ref, xt = refs[N_PAGES:]
    cpp = PAGE_SIZE // CMP_STRIDE
    for p, pg in enumerate(pages):
        xt[:, :, p * cpp:(p + 1) * cpp, :] = pltpu.einshape(
            "nicd->icnd", pg[...].reshape(cpp, CMP_STRIDE, ROW_SPLIT, HEAD_DIM))

    _compress_core(lambda i, c: xt[i, c], wc_ref, w1_ref, w2_ref, cpos_ref, out_ref)


def _compress_weight_specs():
    z3 = lambda *a: (0, 0, 0)
    return [pl.BlockSpec((2, CMP_STRIDE * HEAD_DIM, 2 * HEAD_DIM), z3),
            pl.BlockSpec((2, CMP_BLOCK * HEAD_DIM, HEAD_DIM), z3),
            pl.BlockSpec((2, HEAD_DIM, HEAD_DIM), z3),
            pl.BlockSpec((2, 8, CMP_BLOCK * HEAD_DIM), z3)]


def _compress_prompt(kv_rows, cw):
    return pl.pallas_call(
        _compress_prompt_kernel,
        grid=(BATCH,),
        in_specs=[pl.BlockSpec((SEQ * ROW_SPLIT, HEAD_DIM), lambda b: (b, 0)), *_compress_weight_specs()],
        out_specs=pl.BlockSpec((1, N_CMP, KV_COLS), lambda b: (b, 0, 0)),
        out_shape=jax.ShapeDtypeStruct((BATCH, N_CMP, KV_COLS), BF16),
        compiler_params=_params(("parallel",)),
        name="compress_prompt",
    )(kv_rows, *cw)


def _page_specs():
    return [pl.BlockSpec((PAGE_SIZE * ROW_SPLIT, HEAD_DIM), functools.partial(lambda p, s, pt: (pt[s, p], 0), p))
            for p in range(N_PAGES)]


def _compress_sample(cache_rows, page_table, cw):
    grid_spec = pltpu.PrefetchScalarGridSpec(
        num_scalar_prefetch=1,
        grid=(DEC_BATCH,),
        in_specs=[*_page_specs(), *_compress_weight_specs()],
        out_specs=pl.BlockSpec((1, N_CMP, KV_COLS), lambda s, pt: (s, 0, 0)),
        scratch_shapes=[pltpu.VMEM((CMP_STRIDE, ROW_SPLIT, PAST_LEN // CMP_STRIDE, HEAD_DIM), F32)],
    )
    return pl.pallas_call(
        _compress_sample_kernel,
        grid_spec=grid_spec,
        out_shape=jax.ShapeDtypeStruct((DEC_BATCH, N_CMP, KV_COLS), BF16),
        compiler_params=_params(("arbitrary",)),
        name="compress_sample",
    )(page_table, *([cache_rows] * N_PAGES), *cw)


def _masked_exp(s, mask):
    sm = jnp.where(mask, s, NEG)
    m = jnp.max(sm, axis=-1, keepdims=True)
    e = jnp.where(mask, jnp.exp(sm - m), 0.0)
    return e, jnp.sum(e, axis=-1, keepdims=True)


def _rep4(a):
    return jnp.concatenate([a] * GROUP, axis=0)


def _cmp_and_select(qg, kc, vc, ovt, qpos_col, qpos_row, n_tok):
    nsp = ovt.shape[0]
    s = lax.dot_general(qg, kc, _NT, preferred_element_type=F32)
    n_idx = lax.broadcasted_iota(jnp.int32, (n_tok, N_CMP), 1)
    vis = (n_idx * CMP_STRIDE + (CMP_BLOCK - 1) <= qpos_col) & (n_idx < N_CMP - 1)
    e, l = _masked_exp(s, _rep4(vis))
    p = e * (1.0 / jnp.maximum(l, 1e-30))
    o_c = jnp.dot(p.astype(BF16), vc, preferred_element_type=F32)
    psum = p[0:n_tok] + p[n_tok:2 * n_tok] + p[2 * n_tok:3 * n_tok] + p[3 * n_tok:4 * n_tok]
    if n_tok < 128:
        psum = jnp.concatenate([psum, jnp.zeros((128 - n_tok, N_CMP), F32)], axis=0)
    hi, lo = _split_bf16(psum)
    score = (lax.dot_general(ovt, hi, _NT, preferred_element_type=F32)
             + lax.dot_general(ovt, lo, _NT, preferred_element_type=F32))
    m_idx = lax.broadcasted_iota(jnp.int32, (nsp, 128), 0)
    cur = jnp.right_shift(qpos_row, 6)
    forced = (m_idx == 0) | (m_idx == cur) | (m_idx == cur - 1)
    score = jnp.where(forced, BIG, score)
    score = jnp.where(m_idx > cur, -BIG, score)
    rank = jnp.zeros((nsp, 128), F32)
    for mp in range(nsp):
        row = score[mp:mp + 1, :]
        beats = (row > score) | ((row == score) & (m_idx > mp))
        rank = rank + beats.astype(F32)
    sel_t = ((rank < N_SEL) & (score > -0.5 * BIG)).astype(F32)
    sel_t = jnp.concatenate([sel_t, jnp.zeros((128 - nsp, 128), F32)], axis=0)
    return o_c, sel_t.T.astype(BF16)


def _block_expand(n_keys, key0):
    m_idx = lax.broadcasted_iota(jnp.int32, (128, n_keys), 0)
    c_idx = lax.broadcasted_iota(jnp.int32, (128, n_keys), 1)
    return (m_idx == jnp.right_shift(key0 + c_idx, 6)).astype(BF16)


def _flash(qgs, k_ref, v_ref, cols, lo, hi, mask_fns):
    rows = qgs[0].shape[0]
    ones = jnp.ones((KEY_TILE, HEAD_DIM), BF16)
    voff = N_KV_HEADS * HEAD_DIM

    def step(qg, col, mask_fn, kt, carry):
        m, l, acc = carry
        r0 = pl.multiple_of(kt * KEY_TILE, KEY_TILE)
        k = k_ref[pl.ds(r0, KEY_TILE), col:col + HEAD_DIM]
        v = v_ref[pl.ds(r0, KEY_TILE), voff + col:voff + col + HEAD_DIM]
        s = lax.dot_general(qg, k, _NT, preferred_element_type=F32)
        bias = jnp.where(mask_fn(kt * KEY_TILE), 0.0, NEG)
        sm = s + _rep4(bias)
        m_new = jnp.maximum(m, jnp.max(sm, axis=-1, keepdims=True))
        alpha = jnp.exp(m - m_new)
        e = jnp.exp(sm - m_new).astype(BF16)
        pv = jnp.dot(e, jnp.concatenate([v, ones], axis=1), preferred_element_type=F32)
        return m_new, alpha * l + pv[:, HEAD_DIM:HEAD_DIM + 1], alpha * acc + pv[:, :HEAD_DIM]

    def body(kt, carries):
        return tuple(step(qg, col, fn, kt, c) for qg, col, fn, c in zip(qgs, cols, mask_fns, carries))

    init = (jnp.full((rows, 1), NEG, F32), jnp.zeros((rows, 1), F32), jnp.zeros((rows, HEAD_DIM), F32))
    outs = lax.fori_loop(lo, hi, body, (init,) * len(qgs))
    return [acc * (1.0 / l) for _, l, acc in outs]


def _combine_heads(g, n_tok, o_c, o_s, o_w, gates, o_ref):
    for j in range(GROUP):
        hd = g * GROUP + j
        rows = slice(j * n_tok, (j + 1) * n_tok)
        o = (gates[:, 3 * hd:3 * hd + 1] * o_c[rows] + gates[:, 3 * hd + 1:3 * hd + 2] * o_s[rows]
             + gates[:, 3 * hd + 2:3 * hd + 3] * o_w[rows])
        o_ref[:, hd * HEAD_DIM:(hd + 1) * HEAD_DIM] = o.astype(o_ref.dtype)


def _attn_prompt_kernel(q_ref, ks_ref, kw_ref, kvc_ref, gate_ref, ovt_ref, o_ref):
    b = pl.program_id(0)

    @pl.when(b == BATCH)
    def _():
        o_ref[...] = jnp.zeros_like(o_ref)

    @pl.when(b < BATCH)
    def _():
        _attn_prompt_block(q_ref, ks_ref, kw_ref, kvc_ref, gate_ref, ovt_ref, o_ref)


def _attn_prompt_block(q_ref, ks_ref, kw_ref, kvc_ref, gate_ref, ovt_ref, o_ref):
    qb = pl.program_id(1)
    q0 = qb * Q_BLOCK
    qpos_col = q0 + lax.broadcasted_iota(jnp.int32, (Q_BLOCK, 1), 0)
    qpos_row = q0 + lax.broadcasted_iota(jnp.int32, (1, Q_BLOCK), 1)
    gates = gate_ref[...]
    ovt = ovt_ref[...]
    last_tile = qb // 2 + 1
    c_idx = lax.broadcasted_iota(jnp.int32, (Q_BLOCK, KEY_TILE), 1)
    win_lo = jnp.maximum(qb // 2 - 2, 0)

    def win_mask(key0):
        dist = qpos_col - (key0 + c_idx)
        return (dist >= 0) & (dist <= WINDOW)

    for g0 in range(0, N_KV_HEADS, FLASH_GROUPS):
        gs = range(g0, g0 + FLASH_GROUPS)
        cols = [g * HEAD_DIM for g in gs]
        qgs, o_cs, sel_masks = [], [], []
        for g, col in zip(gs, cols):
            qg = jnp.concatenate([q_ref[:, (g * GROUP + j) * HEAD_DIM:(g * GROUP + j + 1) * HEAD_DIM]
                                  for j in range(GROUP)], axis=0)
            vcol = N_KV_HEADS * HEAD_DIM + col
            o_c, sel = _cmp_and_select(qg, kvc_ref[0, :, col:col + HEAD_DIM], kvc_ref[0, :, vcol:vcol + HEAD_DIM],
                                       ovt, qpos_col, qpos_row, Q_BLOCK)

            def sel_mask(key0, sel=sel):
                chosen = jnp.dot(sel, _block_expand(KEY_TILE, key0), preferred_element_type=F32) > 0.5
                return chosen & (key0 + c_idx <= qpos_col)

            qgs.append(qg)
            o_cs.append(o_c)
            sel_masks.append(sel_mask)
        o_ss = _flash(qgs, ks_ref, ks_ref, cols, 0, last_tile, sel_masks)
        o_ws = _flash(qgs, kw_ref, kw_ref, cols, win_lo, last_tile, [win_mask] * FLASH_GROUPS)
        for g, o_c, o_s, o_w in zip(gs, o_cs, o_ss, o_ws):
            _combine_heads(g, Q_BLOCK, o_c, o_s, o_w, gates, o_ref)


def _attn_prompt(q, kvb_sel, kvb_win, kvc, gates, ovt):
    nqb = SEQ // Q_BLOCK
    n_fill = T_S // Q_BLOCK
    last = BATCH - 1
    qrow = lambda b, i: (jnp.minimum(b, last) * nqb + i, 0)
    orow = lambda b, i: (jnp.where(b < BATCH, b * nqb + i, BATCH * nqb + jnp.minimum(i, n_fill - 1)), 0)
    return pl.pallas_call(
        _attn_prompt_kernel,
        grid=(BATCH + 1, nqb),
        in_specs=[pl.BlockSpec((Q_BLOCK, N_Q_COLS), qrow),
                  pl.BlockSpec((SEQ, KV_COLS), lambda b, i: (jnp.minimum(b, last), 0)),
                  pl.BlockSpec((SEQ, KV_COLS), lambda b, i: (jnp.minimum(b, last), 0)),
                  pl.BlockSpec((1, N_CMP, KV_COLS), lambda b, i: (jnp.minimum(b, last), 0, 0)),
                  pl.BlockSpec((Q_BLOCK, 128), qrow),
                  pl.BlockSpec(ovt.shape, lambda b, i: (0, 0))],
        out_specs=pl.BlockSpec((Q_BLOCK, N_Q_COLS), orow),
        out_shape=jax.ShapeDtypeStruct((T_ALL, N_Q_COLS), BF16),
        compiler_params=_params(("arbitrary", "arbitrary")),
        name="attn_prompt",
    )(q, kvb_sel, kvb_win, kvc, gates, ovt)


TAIL = 128
SEL_KEYS = PAST_LEN + TAIL
WIN_KEYS = WINDOW + TAIL
WIN_ROWS = WINDOW * ROW_SPLIT
NEW_ROWS = DEC_SEQ * ROW_SPLIT


def _attn_sample_kernel(pt_ref, *refs):
    pages = refs[:N_PAGES]
    q_ref, news_ref, neww_ref, win_ref, kvc_ref, gate_ref, ovt_ref, o_ref, nwin_ref, sbuf, wbuf = refs[N_PAGES:]
    def by_group(ref):
        x = ref[...]
        return pltpu.einshape("tcd->ctd", x.reshape(x.shape[0] // ROW_SPLIT, ROW_SPLIT, HEAD_DIM))

    pad = jnp.zeros((ROW_SPLIT, TAIL - DEC_SEQ, HEAD_DIM), F32)
    for p in range(N_PAGES):
        sbuf[:, p * PAGE_SIZE:(p + 1) * PAGE_SIZE, :] = by_group(pages[p]).astype(BF16)
    sbuf[:, PAST_LEN:, :] = jnp.concatenate([by_group(news_ref), pad], axis=1).astype(BF16)
    wbuf[:, :WINDOW, :] = by_group(win_ref).astype(BF16)
    wbuf[:, WINDOW:, :] = jnp.concatenate([by_group(neww_ref), pad], axis=1).astype(BF16)
    nwin_ref[:WIN_ROWS - NEW_ROWS, :] = win_ref[NEW_ROWS:, :]
    nwin_ref[WIN_ROWS - NEW_ROWS:, :] = neww_ref[...]

    tcol = lax.broadcasted_iota(jnp.int32, (DEC_SEQ, 1), 0)
    qpos_col = PAST_LEN + tcol
    qpos_row = PAST_LEN + jnp.minimum(lax.broadcasted_iota(jnp.int32, (1, 128), 1), DEC_SEQ - 1)
    gates = gate_ref[...]
    ovt = ovt_ref[...]
    expand = _block_expand(SEL_KEYS, 0)
    sel_pos = lax.broadcasted_iota(jnp.int32, (DEC_SEQ, SEL_KEYS), 1)
    win_c = lax.broadcasted_iota(jnp.int32, (DEC_SEQ, WIN_KEYS), 1)
    win_mask = _rep4((win_c >= tcol) & (win_c <= WINDOW + tcol))
    for g in range(N_KV_HEADS):
        col = g * HEAD_DIM
        vcol = N_KV_HEADS * HEAD_DIM + col
        qg = jnp.concatenate([q_ref[:, (g * GROUP + j) * HEAD_DIM:(g * GROUP + j + 1) * HEAD_DIM]
                              for j in range(GROUP)], axis=0).astype(BF16)
        o_c, sel = _cmp_and_select(qg, kvc_ref[0, :, col:col + HEAD_DIM], kvc_ref[0, :, vcol:vcol + HEAD_DIM],
                                   ovt, qpos_col, qpos_row, DEC_SEQ)
        chosen = jnp.dot(sel, expand, preferred_element_type=F32)[:DEC_SEQ] > 0.5
        mask = _rep4(chosen & (sel_pos <= qpos_col))
        s = lax.dot_general(qg, sbuf[g], _NT, preferred_element_type=F32)
        e, l = _masked_exp(s, mask)
        o_s = jnp.dot(e.astype(BF16), sbuf[N_KV_HEADS + g], preferred_element_type=F32)
        o_s = o_s / jnp.maximum(l, 1e-30)
        s = lax.dot_general(qg, wbuf[g], _NT, preferred_element_type=F32)
        e, l = _masked_exp(s, win_mask)
        o_w = jnp.dot(e.astype(BF16), wbuf[N_KV_HEADS + g], preferred_element_type=F32)
        o_w = o_w / jnp.maximum(l, 1e-30)
        _combine_heads(g, DEC_SEQ, o_c, o_s, o_w, gates, o_ref)


def _attn_sample(q_s, new_sel, new_win, cache_sel, state_win, page_table, kvc, gates, ovt):
    row0 = T_P // DEC_SEQ
    grid_spec = pltpu.PrefetchScalarGridSpec(
        num_scalar_prefetch=1,
        grid=(DEC_BATCH,),
        in_specs=[*_page_specs(),
                  pl.BlockSpec((DEC_SEQ, N_Q_COLS), lambda s, pt: (s, 0)),
                  pl.BlockSpec((NEW_ROWS, HEAD_DIM), lambda s, pt: (s, 0)),
                  pl.BlockSpec((NEW_ROWS, HEAD_DIM), lambda s, pt: (s, 0)),
                  pl.BlockSpec((WIN_ROWS, HEAD_DIM), lambda s, pt: (s, 0)),
                  pl.BlockSpec((1, N_CMP, KV_COLS), lambda s, pt: (s, 0, 0)),
                  pl.BlockSpec((DEC_SEQ, 128), lambda s, pt: (row0 + s, 0)),
                  pl.BlockSpec(ovt.shape, lambda s, pt: (0, 0))],
        out_specs=[pl.BlockSpec((DEC_SEQ, N_Q_COLS), lambda s, pt: (s, 0)),
                   pl.BlockSpec((WIN_ROWS, HEAD_DIM), lambda s, pt: (s, 0))],
        scratch_shapes=[pltpu.VMEM((ROW_SPLIT, SEL_KEYS, HEAD_DIM), BF16),
                        pltpu.VMEM((ROW_SPLIT, WIN_KEYS, HEAD_DIM), BF16)],
    )
    return pl.pallas_call(
        _attn_sample_kernel,
        grid_spec=grid_spec,
        out_shape=[jax.ShapeDtypeStruct((T_S, N_Q_COLS), F32),
                   jax.ShapeDtypeStruct((DEC_BATCH * WIN_ROWS, HEAD_DIM), F32)],
        compiler_params=_params(("arbitrary",)),
        name="attn_sample",
    )(page_table, *([cache_sel] * N_PAGES), q_s, new_sel, new_win, state_win, kvc, gates, ovt)


def _overlap_t(ns, nsp):
    c0 = jnp.arange(N_CMP)[None, :] * CMP_STRIDE
    s0 = jnp.arange(nsp)[:, None] * SEL_BLOCK
    ov = jnp.minimum(c0 + CMP_BLOCK, s0 + SEL_BLOCK) - jnp.maximum(c0, s0)
    ov = jnp.maximum(ov, 0).astype(F32) / CMP_BLOCK
    ov = jnp.where((jnp.arange(nsp)[:, None] < ns) & (jnp.arange(N_CMP)[None, :] < N_CMP - 1), ov, 0.0)
    return ov.astype(BF16)


def _outproj_kernel(npt, o_ref, w_ref, x_ref, gp, gs, shp, shs, scp, scs, lng, lnb, x1_ref, h1_ref):
    i = pl.program_id(0)
    y = jnp.dot(o_ref[...], w_ref[...], preferred_element_type=F32)
    x1 = _layer_norm(ALPHA * x_ref[...] + _pick(i, npt, gp, gs) * y, lng[...], lnb[...])
    x1_ref[...] = x1
    h1_ref[...] = (x1 * (1.0 + _pick(i, npt, scp, scs)) + _pick(i, npt, shp, shs)).astype(BF16)


def _outproj(o, w_out_bf, x, mp, ms, lng, lnb, tm=256):
    row = pl.BlockSpec((tm, D_MODEL), lambda i: (i, 0))
    vec = pl.BlockSpec((1, D_MODEL), lambda i: (0, 0))
    return pl.pallas_call(
        functools.partial(_outproj_kernel, T_P // tm),
        grid=(T_ALL // tm,),
        in_specs=[row, pl.BlockSpec((N_Q_COLS, D_MODEL), lambda i: (0, 0)), row,
                  *_mod_specs(2, tm), *_mod_specs(3, tm), *_mod_specs(4, tm), vec, vec],
        out_specs=[row, row],
        out_shape=[jax.ShapeDtypeStruct((T_ALL, D_MODEL), F32), jax.ShapeDtypeStruct((T_ALL, D_MODEL), BF16)],
        compiler_params=_params(("parallel",)),
        name="nsa_out_postnorm",
    )(o, w_out_bf, x, mp, ms, mp, ms, mp, ms, lng, lnb)


def _swiglu_up_kernel(x_ref, wg_ref, wu_ref, a_ref):
    x = x_ref[...]
    g = jnp.dot(x, wg_ref[...], preferred_element_type=F32)
    u = jnp.dot(x, wu_ref[...], preferred_element_type=F32)
    a_ref[...] = (g * jax.nn.sigmoid(g) * u).astype(BF16)


def _ffn_up(h, w_up_bf, tm=1024, tn=512):
    nj = D_FF // tn
    return pl.pallas_call(
        _swiglu_up_kernel,
        grid=(T_ALL // tm, nj),
        in_specs=[pl.BlockSpec((tm, D_MODEL), lambda i, j: (i, 0)),
                  pl.BlockSpec((D_MODEL, tn), lambda i, j: (0, j)),
                  pl.BlockSpec((D_MODEL, tn), lambda i, j: (0, nj + j))],
        out_specs=pl.BlockSpec((tm, tn), lambda i, j: (i, j)),
        out_shape=jax.ShapeDtypeStruct((T_ALL, D_FF), BF16),
        compiler_params=_params(("parallel", "arbitrary")),
        name="ffn_up",
    )(h, w_up_bf, w_up_bf)


def _ffn_down_kernel(npt, a_ref, w_ref, x_ref, gp, gs, lng, lnb, x2_ref, acc):
    i = pl.program_id(0)
    k = pl.program_id(1)

    @pl.when(k == 0)
    def _():
        acc[...] = jnp.zeros_like(acc)

    acc[...] += jnp.dot(a_ref[...], w_ref[...], preferred_element_type=F32)

    @pl.when(k == pl.num_programs(1) - 1)
    def _():
        x2_ref[...] = _layer_norm(ALPHA * x_ref[...] + _pick(i, npt, gp, gs) * acc[...], lng[...], lnb[...])


def _ffn_down(a, w_down_bf, x, mp, ms, lng, lnb, tm=512, tk=1408):
    row = pl.BlockSpec((tm, D_MODEL), lambda i, k: (i, 0))
    vec = pl.BlockSpec((1, D_MODEL), lambda i, k: (0, 0))
    return pl.pallas_call(
        functools.partial(_ffn_down_kernel, T_P // tm),
        grid=(T_ALL // tm, D_FF // tk),
        in_specs=[pl.BlockSpec((tm, tk), lambda i, k: (i, k)),
                  pl.BlockSpec((tk, D_MODEL), lambda i, k: (k, 0)),
                  row, *_mod_specs(5, tm), vec, vec],
        out_specs=row,
        out_shape=jax.ShapeDtypeStruct((T_ALL, D_MODEL), F32),
        scratch_shapes=[pltpu.VMEM((tm, D_MODEL), F32)],
        compiler_params=_params(("parallel", "arbitrary")),
        name="ffn_down_postnorm",
    )(a, w_down_bf, x, mp, ms, lng, lnb)


HALO = 16


def _pool_kernel(prompt, tiles_per_seq, n_tiles, *refs):
    i = pl.program_id(0)
    x1_ref, h1_ref, _, lg_ref = refs[-4:]

    @pl.when(i >= n_tiles)
    def _():
        x1_ref[...] = jnp.zeros_like(x1_ref)
        h1_ref[...] = jnp.zeros_like(h1_ref)
        lg_ref[...] = jnp.zeros_like(lg_ref)

    @pl.when(i < n_tiles)
    def _():
        _pool_tile(prompt, tiles_per_seq, *refs)


def _pool_tile(prompt, tiles_per_seq, x_ref, halo_ref, sh_m, sc_m, g_m, sh_f, sc_f, hsh, hsc,
               wp_ref, ps_ref, lng, lnb, wr_hi, wr_lo, br_ref, x1_ref, h1_ref, hm_ref, lg_ref):
    i = pl.program_id(0)
    tm = x_ref.shape[0]
    x = x_ref[...]
    h = x * (1.0 + sc_m[...]) + sh_m[...]
    hm_ref[...] = h
    if prompt:
        hh = halo_ref[...] * (1.0 + hsc[...]) + hsh[...]
        hh = jnp.where(i % tiles_per_seq == 0, 0.0, hh)
        pos = (i % tiles_per_seq) * tm + lax.broadcasted_iota(jnp.int32, (tm, 1), 0)
    else:
        hh = jnp.zeros((HALO, D_MODEL), F32)
    ext = jnp.concatenate([hh, h], axis=0)
    ys = []
    for gi, w in enumerate(POOL_WINDOWS):
        a = ext[:, gi * POOL_GROUP:(gi + 1) * POOL_GROUP]
        s = a
        d = 1
        while d < w:
            s = s + pltpu.roll(s, d, axis=0)
            d *= 2
        s = s[HALO:]
        if prompt:
            inv = 1.0 / jnp.minimum(pos + 1, w).astype(F32)
        else:
            inv = 1.0 / w
        diff = (s * inv - a[HALO:]).astype(BF16)
        ys.append(jnp.dot(diff, wp_ref[gi], preferred_element_type=F32))
    y = jnp.concatenate(ys, axis=1) * ps_ref[...]
    x1 = _layer_norm(ALPHA * x + g_m[...] * y, lng[...], lnb[...])
    x1_ref[...] = x1
    h1 = x1 * (1.0 + sc_f[...]) + sh_f[...]
    h1_ref[...] = h1
    hi, lo = _split_bf16(h1)
    lg_ref[...] = (jnp.dot(hi, wr_hi[...], preferred_element_type=F32)
                   + jnp.dot(hi, wr_lo[...], preferred_element_type=F32)
                   + jnp.dot(lo, wr_hi[...], preferred_element_type=F32) + br_ref[...])


def _pool_call(prompt, x, n, n_out, hm_spec, hm_rows, mods, w_pool_bf, pool_scale, lng, lnb, wr_hi, wr_lo,
               b_router, tm):
    row = pl.BlockSpec((tm, D_MODEL), lambda i: (i, 0))
    vec = pl.BlockSpec((1, D_MODEL), lambda i: (0, 0))
    halo = pl.BlockSpec((HALO, D_MODEL), lambda i: (jnp.maximum(i * (tm // HALO) - 1, 0), 0))
    specs = [s for s, _ in mods]
    arrs = [a for _, a in mods]
    return pl.pallas_call(
        functools.partial(_pool_kernel, prompt, SEQ // tm, n // tm),
        grid=(n_out // tm,),
        in_specs=[row, halo, *specs,
                  pl.BlockSpec((4, POOL_GROUP, POOL_GROUP), lambda i: (0, 0, 0)), vec, vec, vec,
                  pl.BlockSpec((D_MODEL, 128), lambda i: (0, 0)), pl.BlockSpec((D_MODEL, 128), lambda i: (0, 0)),
                  pl.BlockSpec((1, 128), lambda i: (0, 0))],
        out_specs=[row, row, hm_spec, pl.BlockSpec((tm, 128), lambda i: (i, 0))],
        out_shape=[jax.ShapeDtypeStruct((n_out, D_MODEL), F32), jax.ShapeDtypeStruct((n_out, D_MODEL), F32),
                   jax.ShapeDtypeStruct((hm_rows, D_MODEL), F32), jax.ShapeDtypeStruct((n_out, 128), F32)],
        compiler_params=_params(("arbitrary",)),
        name="pool_prompt" if prompt else "pool_sample",
    )(x, x, *arrs, w_pool_bf, pool_scale, lng, lnb, wr_hi, wr_lo, b_router)


_POOL_MOD_CHUNKS = (0, 1, 2, 3, 4, 0, 1)


def _pool_prompt(x, mp, *rest, tm=256):
    tpb = SEQ // tm
    seq = lambda i: jnp.minimum(i // tpb, BATCH - 1)
    mods = [(pl.BlockSpec((None, 1, D_MODEL), functools.partial(lambda k, i: (seq(i), 0, k), k)), mp)
            for k in _POOL_MOD_CHUNKS]
    hm_spec = pl.BlockSpec((tm, D_MODEL), lambda i: (seq(i), 0))
    return _pool_call(True, x, T_P, T_ALL, hm_spec, BATCH * tm, mods, *rest, tm)


def _pool_sample(x_ext, ms_ext, *rest, tm=8 * POOL_EXT):
    n = DEC_BATCH * POOL_EXT
    mods = [(pl.BlockSpec((tm, D_MODEL), functools.partial(lambda k, i: (i, k), k)), ms_ext)
            for k in _POOL_MOD_CHUNKS[:5]]
    mods += [(pl.BlockSpec((HALO, D_MODEL), functools.partial(lambda k, i: (0, k), k)), ms_ext)
             for k in _POOL_MOD_CHUNKS[5:]]
    hm_spec = pl.BlockSpec((tm, D_MODEL), lambda i: (i, 0))
    return _pool_call(False, x_ext, n, n, hm_spec, n, mods, *rest, tm)


N_MOE_BLOCKS = -(-(T_ALL * TOP_K) // MOE_BLOCK) + N_EXPERTS
N_SLOTS = N_MOE_BLOCKS * MOE_BLOCK


def _row_gather_kernel(idx_ref, src_hbm, o_ref, buf, sem):
    i = pl.program_id(0)
    nb = pl.num_programs(0)
    bsz = o_ref.shape[0]

    def issue(blk, slot):
        def body(r, carry):
            row = idx_ref[blk * bsz + r]
            pltpu.make_async_copy(src_hbm.at[pl.ds(row, 1)], buf.at[slot, pl.ds(r, 1)], sem.at[slot]).start()
            return carry
        lax.fori_loop(0, bsz, body, 0, unroll=8)

    @pl.when(i == 0)
    def _():
        issue(0, 0)

    @pl.when(i + 1 < nb)
    def _():
        issue(i + 1, (i + 1) % 2)

    slot = i % 2
    pltpu.make_async_copy(src_hbm.at[pl.ds(0, bsz)], buf.at[slot], sem.at[slot]).wait()
    o_ref[...] = buf[slot].astype(BF16)


def _gather_rows_bf16(src, idx):
    grid_spec = pltpu.PrefetchScalarGridSpec(
        num_scalar_prefetch=1,
        grid=(N_MOE_BLOCKS,),
        in_specs=[pl.BlockSpec(memory_space=pl.ANY)],
        out_specs=pl.BlockSpec((MOE_BLOCK, D_MODEL), lambda i, idx: (i, 0)),
        scratch_shapes=[pltpu.VMEM((2, MOE_BLOCK, D_MODEL), F32), pltpu.SemaphoreType.DMA((2,))],
    )
    return pl.pallas_call(
        _row_gather_kernel,
        grid_spec=grid_spec,
        out_shape=jax.ShapeDtypeStruct((N_SLOTS, D_MODEL), BF16),
        compiler_params=_params(("arbitrary",)),
        name="moe_dispatch_gather",
    )(idx, src)


def _moe_up_kernel(be_ref, fresh_ref, live_ref, x_ref, wg_ref, wu_ref, a_ref, wg_s, wu_s):
    i = pl.program_id(1)

    @pl.when(fresh_ref[i] == 1)
    def _():
        wg_s[...] = wg_ref[...].astype(BF16)
        wu_s[...] = wu_ref[...].astype(BF16)

    @pl.when(live_ref[i] == 1)
    def _():
        x = x_ref[...]
        g = jnp.dot(x, wg_s[...], preferred_element_type=F32)
        u = jnp.dot(x, wu_s[...], preferred_element_type=F32)
        a_ref[...] = (g * jax.nn.sigmoid(g) * u).astype(BF16)

    @pl.when(live_ref[i] == 0)
    def _():
        a_ref[...] = jnp.zeros_like(a_ref)


def _moe_up(xb, w_up, block_expert, fresh, live, tn=1024):
    nj = D_FF_EXPERT // tn
    grid_spec = pltpu.PrefetchScalarGridSpec(
        num_scalar_prefetch=3,
        grid=(nj, N_MOE_BLOCKS),
        in_specs=[pl.BlockSpec((MOE_BLOCK, D_MODEL), lambda j, i, be, fr, lv: (i, 0)),
                  pl.BlockSpec((None, D_MODEL, tn), lambda j, i, be, fr, lv: (be[i], 0, j)),
                  pl.BlockSpec((None, D_MODEL, tn), lambda j, i, be, fr, lv: (be[i], 0, nj + j))],
        out_specs=pl.BlockSpec((MOE_BLOCK, tn), lambda j, i, be, fr, lv: (i, j)),
        scratch_shapes=[pltpu.VMEM((D_MODEL, tn), BF16), pltpu.VMEM((D_MODEL, tn), BF16)],
    )
    return pl.pallas_call(
        _moe_up_kernel,
        grid_spec=grid_spec,
        out_shape=jax.ShapeDtypeStruct((N_SLOTS, D_FF_EXPERT), BF16),
        compiler_params=_params(("arbitrary", "arbitrary")),
        name="moe_up",
    )(block_expert, fresh, live, xb, w_up, w_up)


def _moe_down_kernel(be_ref, fresh_ref, live_ref, a_ref, w_ref, sg_ref, y_ref, w_s):
    i = pl.program_id(1)

    @pl.when(fresh_ref[i] == 1)
    def _():
        w_s[...] = w_ref[...].astype(BF16)

    @pl.when(live_ref[i] == 1)
    def _():
        y_ref[...] = jnp.dot(a_ref[...], w_s[...], preferred_element_type=F32) * sg_ref[...]

    @pl.when(live_ref[i] == 0)
    def _():
        y_ref[...] = jnp.zeros_like(y_ref)


def _moe_down(act, w_down, slot_gate, block_expert, fresh, live, tn=512):
    grid_spec = pltpu.PrefetchScalarGridSpec(
        num_scalar_prefetch=3,
        grid=(D_MODEL // tn, N_MOE_BLOCKS),
        in_specs=[pl.BlockSpec((MOE_BLOCK, D_FF_EXPERT), lambda j, i, be, fr, lv: (i, 0)),
                  pl.BlockSpec((None, D_FF_EXPERT, tn), lambda j, i, be, fr, lv: (be[i], 0, j)),
                  pl.BlockSpec((MOE_BLOCK, 1), lambda j, i, be, fr, lv: (i, 0))],
        out_specs=pl.BlockSpec((MOE_BLOCK, tn), lambda j, i, be, fr, lv: (i, j)),
        scratch_shapes=[pltpu.VMEM((D_FF_EXPERT, tn), BF16)],
    )
    return pl.pallas_call(
        _moe_down_kernel,
        grid_spec=grid_spec,
        out_shape=jax.ShapeDtypeStruct((N_SLOTS, D_MODEL), F32),
        compiler_params=_params(("arbitrary", "arbitrary")),
        name="moe_down",
    )(block_expert, fresh, live, act, w_down, slot_gate)


def _route(logits):
    n = logits.shape[0]
    top_logit, top_e = lax.top_k(logits, TOP_K)
    gate = jax.nn.softmax(top_logit, axis=-1)
    e_flat = top_e.reshape(-1)
    order = jnp.argsort(e_flat, stable=True)
    e_sorted = e_flat[order]
    counts = jnp.bincount(e_flat, length=N_EXPERTS)
    padded = (counts + MOE_BLOCK - 1) // MOE_BLOCK * MOE_BLOCK
    start = jnp.cumsum(counts) - counts
    pend = jnp.cumsum(padded)
    pstart = pend - padded
    dest = (pstart[e_sorted] + jnp.arange(n * TOP_K) - start[e_sorted]).astype(jnp.int32)
    blk0 = jnp.arange(N_MOE_BLOCKS) * MOE_BLOCK
    block_expert = jnp.minimum(jnp.searchsorted(pend, blk0, side='right'), N_EXPERTS - 1).astype(jnp.int32)
    live = (blk0 < pend[-1]).astype(jnp.int32)
    fresh = jnp.concatenate([jnp.ones((1,), jnp.int32),
                             (block_expert[1:] != block_expert[:-1]).astype(jnp.int32)])
    slot = jnp.arange(N_SLOTS)
    se = jnp.repeat(block_expert, MOE_BLOCK)
    off = slot - pstart[se]
    valid = off < counts[se]
    src = jnp.clip(start[se] + off, 0, n * TOP_K - 1)
    flat = order[src]
    slot_tok = jnp.where(valid, flat // TOP_K, n - 1).astype(jnp.int32)
    slot_gate = jnp.where(valid, gate.reshape(-1)[flat], 0.0)
    slot_of = dest[jnp.argsort(order)].reshape(n, TOP_K)
    return slot_tok, slot_gate, block_expert, fresh, live, slot_of


def _final_kernel(npt, sa_ref, sb_ref, x_ref, y_hbm, gp, gs, lng, lnb, op_ref, os_ref, buf, sem):
    i = pl.program_id(0)
    nb = pl.num_programs(0)
    tm = x_ref.shape[0]

    def issue(blk, slot):
        def body(r, carry):
            t = blk * tm + r
            pltpu.make_async_copy(y_hbm.at[pl.ds(sa_ref[t], 1)], buf.at[slot, 0, pl.ds(r, 1)], sem.at[slot]).start()
            pltpu.make_async_copy(y_hbm.at[pl.ds(sb_ref[t], 1)], buf.at[slot, 1, pl.ds(r, 1)], sem.at[slot]).start()
            return carry
        lax.fori_loop(0, tm, body, 0, unroll=8)

    @pl.when(i == 0)
    def _():
        issue(0, 0)

    @pl.when(i + 1 < nb)
    def _():
        issue(i + 1, (i + 1) % 2)

    slot = i % 2
    for half in range(TOP_K):
        pltpu.make_async_copy(y_hbm.at[pl.ds(0, tm)], buf.at[slot, half], sem.at[slot]).wait()
    y = buf[slot, 0] + buf[slot, 1]
    out = _layer_norm(ALPHA * x_ref[...] + _pick(i, npt, gp, gs) * y, lng[...], lnb[...])

    @pl.when(i < npt)
    def _():
        op_ref[...] = out

    @pl.when(i >= npt)
    def _():
        os_ref[...] = out


def _final(x, yb, slot_a, slot_b, mp, ms, lng, lnb, tm=256):
    npt = T_P // tm
    row = pl.BlockSpec((tm, D_MODEL), lambda i, *_: (i, 0))
    vec = pl.BlockSpec((1, D_MODEL), lambda i, *_: (0, 0))
    grid_spec = pltpu.PrefetchScalarGridSpec(
        num_scalar_prefetch=2,
        grid=(T_ALL // tm,),
        in_specs=[row, pl.BlockSpec(memory_space=pl.ANY), *_mod_specs(5, tm), vec, vec],
        out_specs=[pl.BlockSpec((tm, D_MODEL), lambda i, *_: (jnp.minimum(i, npt - 1), 0)),
                   pl.BlockSpec((tm, D_MODEL), lambda i, *_: (jnp.maximum(i - npt, 0), 0))],
        scratch_shapes=[pltpu.VMEM((2, 2, tm, D_MODEL), F32), pltpu.SemaphoreType.DMA((2,))],
    )
    return pl.pallas_call(
        functools.partial(_final_kernel, npt),
        grid_spec=grid_spec,
        out_shape=[jax.ShapeDtypeStruct((T_P, D_MODEL), F32), jax.ShapeDtypeStruct((T_S, D_MODEL), F32)],
        compiler_params=_params(("arbitrary",)),
        name="moe_combine_postnorm",
    )(slot_a, slot_b, x, yb, mp, ms, lng, lnb)


def kernel(x_prompt, x_sample, cache_cmp_kv, cache_sel_kv, state_win_kv, state_pool, page_table,
           c_prompt, c_sample, w_ada, b_ada, ln_gain, ln_bias, w_nsa_in, w_cmp1, w_cmp2, cmp_pos,
           w_nsa_out, w_pool, pool_scale, w_ffn_up, w_ffn_down, w_router, b_router, w_moe_up, w_moe_down):
    x = jnp.concatenate([x_prompt.reshape(T_P, D_MODEL), x_sample.reshape(T_S, D_MODEL)], axis=0)
    mods = _ada(jnp.concatenate([c_prompt, c_sample], axis=0), w_ada, b_ada)

    def layer_mods(i):
        return mods[i, :BATCH].reshape(BATCH, 1, 6 * D_MODEL), jnp.repeat(mods[i, BATCH:], DEC_SEQ, axis=0)

    mp, ms = layer_mods(0)
    lng = ln_gain[0].reshape(2, 1, D_MODEL)
    lnb = ln_bias[0].reshape(2, 1, D_MODEL)
    h = _modulate(x, mp, ms, 0, 1)
    w_in_bf = w_nsa_in[0].astype(BF16)
    w_gate_bf = jnp.pad(w_in_bf[:, N_Q_COLS + 3 * KV_COLS:], ((0, 0), (0, 128 - N_GATE_COLS)))
    q, (cmp_kv, sel_kv, win_kv), gates = _nsa_project(h, w_in_bf, w_gate_bf)

    w1 = w_cmp1[0].astype(BF16)
    wc = w1.reshape(2, 2, CMP_STRIDE, HEAD_DIM, HEAD_DIM).transpose(0, 2, 3, 1, 4).reshape(
        2, CMP_STRIDE * HEAD_DIM, 2 * HEAD_DIM)
    cpos = jnp.pad(cmp_pos[0].reshape(2, 1, CMP_BLOCK * HEAD_DIM), ((0, 0), (0, 7), (0, 0)))
    cw = (wc, w1, w_cmp2[0].astype(BF16), cpos)
    stored = lambda a: a.reshape(-1, HEAD_DIM)
    kvc_p = _compress_prompt(cmp_kv[0], cw)
    kvc_s = _compress_sample(stored(cache_cmp_kv), page_table, cw)

    o = _attn_prompt(q, sel_kv[2], win_kv[2], kvc_p, gates, _overlap_t(SEQ // SEL_BLOCK, 32))
    ns_s = -(-(PAST_LEN + DEC_SEQ) // SEL_BLOCK)
    o_s, new_win_rows = _attn_sample(q[T_P:].astype(F32), sel_kv[1], win_kv[1], stored(cache_sel_kv),
                                     stored(state_win_kv), page_table, kvc_s, gates, _overlap_t(ns_s, 40))
    o = lax.dynamic_update_slice(o, o_s.astype(BF16), (T_P, 0))
    x1, h1 = _outproj(o, w_nsa_out[0].astype(BF16), x, mp, ms, lng[0], lnb[0])
    act = _ffn_up(h1, w_ffn_up[0].astype(BF16))
    x2 = _ffn_down(act, w_ffn_down[0].astype(BF16), x1, mp, ms, lng[1], lnb[1])

    kv_p = lambda a: a.reshape(1, BATCH, SEQ, 2, N_KV_HEADS, HEAD_DIM)
    kv_s = lambda a: a.reshape(1, DEC_BATCH, -1, 2, N_KV_HEADS, HEAD_DIM)
    new_cmp_p, new_cmp_s = kv_p(cmp_kv[0]), kv_s(cmp_kv[1])
    new_sel_p, new_sel_s = kv_p(sel_kv[0]), kv_s(sel_kv[1])
    new_win_p = kv_p(win_kv[0])[:, :, SEQ - WINDOW:]
    new_win_s = kv_s(new_win_rows)

    mp, ms = layer_mods(1)
    lng = ln_gain[1].reshape(2, 1, D_MODEL)
    lnb = ln_bias[1].reshape(2, 1, D_MODEL)
    wr_pad = jnp.pad(w_router[0], ((0, 0), (0, 128 - N_EXPERTS)))
    wr_hi, wr_lo = _split_bf16(wr_pad)
    br = jnp.pad(b_router[0].reshape(1, N_EXPERTS), ((0, 0), (0, 128 - N_EXPERTS)))
    pool_args = (w_pool[0].astype(BF16), pool_scale[0].reshape(1, D_MODEL), lng[0], lnb[0], wr_hi, wr_lo, br)
    x3, h3, hmp, lg = _pool_prompt(x2, mp, *pool_args)

    ms3 = ms.reshape(DEC_BATCH, DEC_SEQ, 6 * D_MODEL)
    ms_ext = jnp.pad(ms3, ((0, 0), (POOL_EXT - DEC_SEQ, 0), (0, 0))).reshape(DEC_BATCH * POOL_EXT, 6 * D_MODEL)
    xs_ext = jnp.concatenate([jnp.zeros((DEC_BATCH, 1, D_MODEL), F32), state_pool[0],
                              x2[T_P:].reshape(DEC_BATCH, DEC_SEQ, D_MODEL)], axis=1)
    xs_ext = xs_ext.reshape(DEC_BATCH * POOL_EXT, D_MODEL)
    x1s, h1s, hms, lgs = _pool_sample(xs_ext, ms_ext, *pool_args)
    new_rows = lambda a: a.reshape(DEC_BATCH, POOL_EXT, -1)[:, POOL_EXT - DEC_SEQ:].reshape(T_S, -1)
    x3 = lax.dynamic_update_slice(x3, new_rows(x1s), (T_P, 0))
    h3 = lax.dynamic_update_slice(h3, new_rows(h1s), (T_P, 0))
    logits = lax.dynamic_update_slice(lg, new_rows(lgs), (T_P, 0))[:, :N_EXPERTS]
    new_pool_p = hmp.reshape(BATCH, -1, D_MODEL)[None, :, -POOL_BUF:]
    new_pool_s = hms.reshape(DEC_BATCH, POOL_EXT, D_MODEL)[None, :, POOL_EXT - POOL_BUF:]

    slot_tok, slot_gate, block_expert, fresh, live, slot_of = _route(logits)
    xb = _gather_rows_bf16(h3, slot_tok)
    act = _moe_up(xb, w_moe_up[0], block_expert, fresh, live)
    yb = _moe_down(act, w_moe_down[0], slot_gate.reshape(-1, 1), block_expert, fresh, live)
    y_prompt, y_sample = _final(x3, yb, slot_of[:, 0], slot_of[:, 1], mp, ms, lng[1], lnb[1])

    return (y_prompt.reshape(BATCH, SEQ, D_MODEL), y_sample.reshape(DEC_BATCH, DEC_SEQ, D_MODEL),
            new_cmp_p, new_cmp_s, new_sel_p, new_sel_s, new_win_p, new_win_s, new_pool_p, new_pool_s)
```

```python
import functools

import jax
import jax.numpy as jnp
from jax import lax
from jax.experimental import pallas as pl
from jax.experimental.pallas import tpu as pltpu

F32 = jnp.float32
BF16 = jnp.bfloat16

D_MODEL = 2048
BATCH = 8
SEQ = 2048
DEPTH = 2
DEC_BATCH = 128
DEC_SEQ = 8
PAST_LEN = 2048
PAGE_SIZE = 128
N_PAGES = PAST_LEN // PAGE_SIZE
N_HEADS = 16
HEAD_DIM = 128
N_KV_HEADS = 4
GROUP = 4
CMP_BLOCK = 32
CMP_STRIDE = 16
SEL_BLOCK = 64
N_SEL = 16
WINDOW = 512
Q_BLOCK = 128
SCALE = HEAD_DIM ** -0.5
KV_COLS = 2 * N_KV_HEADS * HEAD_DIM
ROW_SPLIT = 2 * N_KV_HEADS
N_Q_COLS = N_HEADS * HEAD_DIM
N_GATE_COLS = N_HEADS * 3
POOL_WINDOWS = (2, 4, 8, 16)
POOL_GROUP = D_MODEL // 4
POOL_BUF = 15
POOL_EXT = 24
D_FF = 11 * D_MODEL // 4
N_EXPERTS = 8
TOP_K = 2
D_FF_EXPERT = 7 * D_MODEL // 2
MOE_BLOCK = 256
ALPHA = (2.0 * DEPTH) ** 0.25
LN_EPS = 1e-5
BIG = 1e9
NEG = -1e30

T_P = BATCH * SEQ
T_S = DEC_BATCH * DEC_SEQ
T_ALL = T_P + T_S
N_CMP = 128
SEL_TILE = 512
WIN_TILE = 256
FLASH_GROUPS = 2
VMEM_LIMIT = 56 * 1024 * 1024

_NT = (((1,), (1,)), ((), ()))


def _params(sem, vmem=VMEM_LIMIT):
    return pltpu.CompilerParams(dimension_semantics=sem, vmem_limit_bytes=vmem)


def _layer_norm(z, g, b):
    mu = jnp.mean(z, axis=-1, keepdims=True)
    zc = z - mu
    var = jnp.mean(zc * zc, axis=-1, keepdims=True)
    return zc * lax.rsqrt(var + LN_EPS) * g + b


def _pick(i, n_prompt_tiles, p_ref, s_ref):
    return jnp.where(i < n_prompt_tiles, p_ref[...], s_ref[...])


def _mod_specs(k, tm):
    tpb = SEQ // tm
    npt = T_P // tm
    return (pl.BlockSpec((None, 1, D_MODEL), lambda i, *_: (jnp.minimum(i // tpb, BATCH - 1), 0, k)),
            pl.BlockSpec((tm, D_MODEL), lambda i, *_: (jnp.maximum(i - npt, 0), k)))


def _split_bf16(x):
    hi = x.astype(BF16)
    lo = (x - hi.astype(F32)).astype(BF16)
    return hi, lo


def _ada_kernel(c_ref, w_ref, b_ref, o_ref):
    c = c_ref[...]
    a = (c * jax.nn.sigmoid(c)).astype(BF16)
    o_ref[0] = jnp.dot(a, w_ref[0].astype(BF16), preferred_element_type=F32) + b_ref[0]


def _ada(c_all, w_ada, b_ada):
    nb = c_all.shape[0]
    tn = 1024
    return pl.pallas_call(
        _ada_kernel,
        grid=(DEPTH, 6 * D_MODEL // tn),
        in_specs=[pl.BlockSpec((nb, D_MODEL), lambda l, j: (0, 0)),
                  pl.BlockSpec((1, D_MODEL, tn), lambda l, j: (l, 0, j)),
                  pl.BlockSpec((1, 1, tn), lambda l, j: (l, 0, j))],
        out_specs=pl.BlockSpec((1, nb, tn), lambda l, j: (l, 0, j)),
        out_shape=jax.ShapeDtypeStruct((DEPTH, nb, 6 * D_MODEL), F32),
        compiler_params=_params(("arbitrary", "arbitrary")),
        name="adaln",
    )(c_all, w_ada, b_ada.reshape(DEPTH, 1, 6 * D_MODEL))


def _x_specs(tm):
    npt = T_P // tm
    return (pl.BlockSpec((tm, D_MODEL), lambda i: (jnp.minimum(i, npt - 1), 0)),
            pl.BlockSpec((tm, D_MODEL), lambda i: (jnp.maximum(i - npt, 0), 0)))


def _modulate_kernel(npt, xp_ref, xs_ref, shp, shs, scp, scs, h_ref):
    i = pl.program_id(0)
    shift = _pick(i, npt, shp, shs)
    scale = _pick(i, npt, scp, scs)
    h_ref[...] = (_pick(i, npt, xp_ref, xs_ref) * (1.0 + scale) + shift).astype(BF16)


def _modulate(xp, xs, mp, ms, k_shift, k_scale, tm=256):
    return pl.pallas_call(
        functools.partial(_modulate_kernel, T_P // tm),
        grid=(T_ALL // tm,),
        in_specs=[*_x_specs(tm), *_mod_specs(k_shift, tm), *_mod_specs(k_scale, tm)],
        out_specs=pl.BlockSpec((tm, D_MODEL), lambda i: (i, 0)),
        out_shape=jax.ShapeDtypeStruct((T_ALL, D_MODEL), BF16),
        compiler_params=_params(("parallel",)),
        name="modulate",
    )(xp, xs, mp, ms, mp, ms)


def _proj_q_kernel(x_ref, w_ref, q_ref):
    q_ref[...] = (jnp.dot(x_ref[...], w_ref[...], preferred_element_type=F32) * SCALE).astype(BF16)


def _store_kv_rows(ref, r):
    for c in range(ROW_SPLIT):
        ref[pl.ds(c, r.shape[0], stride=ROW_SPLIT), :] = r[:, c * HEAD_DIM:(c + 1) * HEAD_DIM]


def _proj_kv_kernel(npt, x_ref, w_ref, kvp_ref, kvs_ref, kvb_ref):
    i = pl.program_id(0)
    r = jnp.dot(x_ref[...], w_ref[...], preferred_element_type=F32)
    kvb_ref[...] = r.astype(BF16)

    @pl.when(i < npt)
    def _():
        _store_kv_rows(kvp_ref, r)

    @pl.when(i >= npt)
    def _():
        _store_kv_rows(kvs_ref, r)


def _proj_gate_kernel(x_ref, w_ref, g_ref):
    g_ref[...] = jax.nn.sigmoid(jnp.dot(x_ref[...], w_ref[...], preferred_element_type=F32))


def _nsa_project(h, w_in_bf, w_gate_bf, tm=1024, tn=1024):
    nm = T_ALL // tm
    q = pl.pallas_call(
        _proj_q_kernel,
        grid=(nm, N_Q_COLS // tn),
        in_specs=[pl.BlockSpec((tm, D_MODEL), lambda i, j: (i, 0)),
                  pl.BlockSpec((D_MODEL, tn), lambda i, j: (0, j))],
        out_specs=pl.BlockSpec((tm, tn), lambda i, j: (i, j)),
        out_shape=jax.ShapeDtypeStruct((T_ALL, N_Q_COLS), BF16),
        compiler_params=_params(("parallel", "arbitrary")),
        name="nsa_proj_q",
    )(h, w_in_bf)
    npt = T_P // tm
    rows = tm * ROW_SPLIT
    branches = []
    for br in range(3):
        wcol = N_Q_COLS // KV_COLS + br
        branches.append(pl.pallas_call(
            functools.partial(_proj_kv_kernel, npt),
            grid=(nm,),
            in_specs=[pl.BlockSpec((tm, D_MODEL), lambda i: (i, 0)),
                      pl.BlockSpec((D_MODEL, KV_COLS), functools.partial(lambda c, i: (0, c), wcol))],
            out_specs=[pl.BlockSpec((rows, HEAD_DIM), lambda i: (jnp.minimum(i, npt - 1), 0)),
                       pl.BlockSpec((rows, HEAD_DIM), lambda i: (jnp.maximum(i - npt, 0), 0)),
                       pl.BlockSpec((tm, KV_COLS), lambda i: (i, 0))],
            out_shape=[jax.ShapeDtypeStruct((T_P * ROW_SPLIT, HEAD_DIM), F32),
                       jax.ShapeDtypeStruct((T_S * ROW_SPLIT, HEAD_DIM), F32),
                       jax.ShapeDtypeStruct((T_ALL, KV_COLS), BF16)],
            compiler_params=_params(("arbitrary",)),
            name=f"nsa_proj_kv{br}",
        )(h, w_in_bf))
    gates = pl.pallas_call(
        _proj_gate_kernel,
        grid=(nm,),
        in_specs=[pl.BlockSpec((tm, D_MODEL), lambda i: (i, 0)),
                  pl.BlockSpec((D_MODEL, 128), lambda i: (0, 0))],
        out_specs=pl.BlockSpec((tm, 128), lambda i: (i, 0)),
        out_shape=jax.ShapeDtypeStruct((T_ALL, 128), F32),
        compiler_params=_params(("parallel",)),
        name="nsa_proj_gate",
    )(h, w_gate_bf)
    return q, branches, gates


CHUNK_PITCH = CMP_STRIDE * ROW_SPLIT


def _gelu_tanh(x):
    return 0.5 * x * (1.0 + jnp.tanh(0.7978845608028654 * (x + 0.044715 * x * x * x)))


def _compress_core(load, wc_ref, w1_ref, w2_ref, cpos_ref, out_ref):
    n_chunk = PAST_LEN // CMP_STRIDE
    for k2 in range(2):
        cols = []
        for i in range(CMP_STRIDE):
            heads = [load(i, k2 * N_KV_HEADS + h) for h in range(N_KV_HEADS)]
            cols.append(jnp.concatenate(heads, axis=0).astype(BF16))
        x = jnp.concatenate(cols, axis=1)
        part = jnp.dot(x, wc_ref[k2], preferred_element_type=F32)
        nrow = N_KV_HEADS * n_chunk
        hsum = part[:, :HEAD_DIM] + pltpu.roll(part[:, HEAD_DIM:], nrow - 1, axis=0)
        pos_bias = jnp.dot(cpos_ref[k2].astype(BF16), w1_ref[k2], preferred_element_type=F32)[0:1]
        hid = _gelu_tanh(hsum + pos_bias)
        kvc = jnp.dot(hid.astype(BF16), w2_ref[k2], preferred_element_type=F32)
        for h in range(N_KV_HEADS):
            c0 = (k2 * N_KV_HEADS + h) * HEAD_DIM
            out_ref[0, :, c0:c0 + HEAD_DIM] = kvc[h * n_chunk:(h + 1) * n_chunk].astype(BF16)


def _compress_prompt_kernel(x_ref, wc_ref, w1_ref, w2_ref, cpos_ref, out_ref):
    n_chunk = SEQ // CMP_STRIDE

    def load(i, c):
        return x_ref[pl.ds(ROW_SPLIT * i + c, n_chunk, stride=CHUNK_PITCH), :]

    _compress_core(load, wc_ref, w1_ref, w2_ref, cpos_ref, out_ref)


def _compress_sample_kernel(pt_ref, *refs):
    pages = refs[:N_PAGES]
    wc_ref, w1_ref, w2_ref, cpos_ref, out_ref, xt = refs[N_PAGES:]
    cpp = PAGE_SIZE // CMP_STRIDE
    for p, pg in enumerate(pages):
        xt[:, :, p * cpp:(p + 1) * cpp, :] = pltpu.einshape(
            "nicd->icnd", pg[...].reshape(cpp, CMP_STRIDE, ROW_SPLIT, HEAD_DIM))

    _compress_core(lambda i, c: xt[i, c], wc_ref, w1_ref, w2_ref, cpos_ref, out_ref)


def _compress_weight_specs():
    z3 = lambda *a: (0, 0, 0)
    return [pl.BlockSpec((2, CMP_STRIDE * HEAD_DIM, 2 * HEAD_DIM), z3),
            pl.BlockSpec((2, CMP_BLOCK * HEAD_DIM, HEAD_DIM), z3),
            pl.BlockSpec((2, HEAD_DIM, HEAD_DIM), z3),
            pl.BlockSpec((2, 8, CMP_BLOCK * HEAD_DIM), z3)]


def _compress_prompt(kv_rows, cw):
    return pl.pallas_call(
        _compress_prompt_kernel,
        grid=(BATCH,),
        in_specs=[pl.BlockSpec((SEQ * ROW_SPLIT, HEAD_DIM), lambda b: (b, 0)), *_compress_weight_specs()],
        out_specs=pl.BlockSpec((1, N_CMP, KV_COLS), lambda b: (b, 0, 0)),
        out_shape=jax.ShapeDtypeStruct((BATCH, N_CMP, KV_COLS), BF16),
        compiler_params=_params(("parallel",)),
        name="compress_prompt",
    )(kv_rows, *cw)


def _page_specs():
    return [pl.BlockSpec((PAGE_SIZE * ROW_SPLIT, HEAD_DIM), functools.partial(lambda p, s, pt: (pt[s, p], 0), p))
            for p in range(N_PAGES)]


def _compress_sample(cache_rows, page_table, cw):
    grid_spec = pltpu.PrefetchScalarGridSpec(
        num_scalar_prefetch=1,
        grid=(DEC_BATCH,),
        in_specs=[*_page_specs(), *_compress_weight_specs()],
        out_specs=pl.BlockSpec((1, N_CMP, KV_COLS), lambda s, pt: (s, 0, 0)),
        scratch_shapes=[pltpu.VMEM((CMP_STRIDE, ROW_SPLIT, PAST_LEN // CMP_STRIDE, HEAD_DIM), F32)],
    )
    return pl.pallas_call(
        _compress_sample_kernel,
        grid_spec=grid_spec,
        out_shape=jax.ShapeDtypeStruct((DEC_BATCH, N_CMP, KV_COLS), BF16),
        compiler_params=_params(("arbitrary",)),
        name="compress_sample",
    )(page_table, *([cache_rows] * N_PAGES), *cw)


def _masked_exp(s, mask):
    sm = jnp.where(mask, s, NEG)
    m = jnp.max(sm, axis=-1, keepdims=True)
    e = jnp.where(mask, jnp.exp(sm - m), 0.0)
    return e, jnp.sum(e, axis=-1, keepdims=True)


def _rep4(a):
    return jnp.concatenate([a] * GROUP, axis=0)


def _cmp_and_select(qg, kc, vc, ovt, qpos_col, qpos_row, n_tok):
    nsp = ovt.shape[0]
    s = lax.dot_general(qg, kc, _NT, preferred_element_type=F32)
    n_idx = lax.broadcasted_iota(jnp.int32, (n_tok, N_CMP), 1)
    vis = (n_idx * CMP_STRIDE + (CMP_BLOCK - 1) <= qpos_col) & (n_idx < N_CMP - 1)
    e, l = _masked_exp(s, _rep4(vis))
    p = e * (1.0 / jnp.maximum(l, 1e-30))
    o_c = jnp.dot(p.astype(BF16), vc, preferred_element_type=F32)
    psum = p[0:n_tok] + p[n_tok:2 * n_tok] + p[2 * n_tok:3 * n_tok] + p[3 * n_tok:4 * n_tok]
    if n_tok < 128:
        psum = jnp.concatenate([psum, jnp.zeros((128 - n_tok, N_CMP), F32)], axis=0)
    hi, lo = _split_bf16(psum)
    score = (lax.dot_general(ovt, hi, _NT, preferred_element_type=F32)
             + lax.dot_general(ovt, lo, _NT, preferred_element_type=F32))
    m_idx = lax.broadcasted_iota(jnp.int32, (nsp, 128), 0)
    cur = jnp.right_shift(qpos_row, 6)
    forced = (m_idx == 0) | (m_idx == cur) | (m_idx == cur - 1)
    score = jnp.where(forced, BIG, score)
    score = jnp.where(m_idx > cur, -BIG, score)
    rank = jnp.zeros((nsp, 128), F32)
    for mp in range(nsp):
        row = score[mp:mp + 1, :]
        beats = (row > score) | ((row == score) & (m_idx > mp))
        rank = rank + beats.astype(F32)
    sel_t = ((rank < N_SEL) & (score > -0.5 * BIG)).astype(F32)
    sel_t = jnp.concatenate([sel_t, jnp.zeros((128 - nsp, 128), F32)], axis=0)
    return o_c, sel_t.T.astype(BF16)


def _block_expand(n_keys, key0):
    m_idx = lax.broadcasted_iota(jnp.int32, (128, n_keys), 0)
    c_idx = lax.broadcasted_iota(jnp.int32, (128, n_keys), 1)
    return (m_idx == jnp.right_shift(key0 + c_idx, 6)).astype(BF16)


def _flash(qgs, k_ref, v_ref, cols, lo, hi, mask_fns, tile):
    rows = qgs[0].shape[0]
    ones = jnp.ones((tile, HEAD_DIM), BF16)
    voff = N_KV_HEADS * HEAD_DIM

    def step(qg, col, mask_fn, kt, carry):
        m, l, acc = carry
        r0 = pl.multiple_of(kt * tile, tile)
        k = k_ref[pl.ds(r0, tile), col:col + HEAD_DIM]
        v = v_ref[pl.ds(r0, tile), voff + col:voff + col + HEAD_DIM]
        s = lax.dot_general(qg, k, _NT, preferred_element_type=F32)
        bias = jnp.where(mask_fn(kt * tile), 0.0, NEG)
        sm = s + _rep4(bias)
        m_new = jnp.maximum(m, jnp.max(sm, axis=-1, keepdims=True))
        alpha = jnp.exp(m - m_new)
        e = jnp.exp(sm - jnp.concatenate([m_new] * (tile // HEAD_DIM), axis=1)).astype(BF16)
        pv = jnp.dot(e, jnp.concatenate([v, ones], axis=1), preferred_element_type=F32)
        return m_new, alpha * l + pv[:, HEAD_DIM:], alpha * acc + pv[:, :HEAD_DIM]

    def body(kt, carries):
        return tuple(step(qg, col, fn, kt, c) for qg, col, fn, c in zip(qgs, cols, mask_fns, carries))

    zero = jnp.zeros((rows, HEAD_DIM), F32)
    outs = lax.fori_loop(lo, hi, body, ((jnp.full((rows, HEAD_DIM), NEG, F32), zero, zero),) * len(qgs))
    return [acc * (1.0 / l) for _, l, acc in outs]


def _combine_heads(g, n_tok, o_c, o_s, o_w, gates, o_ref):
    for j in range(GROUP):
        hd = g * GROUP + j
        rows = slice(j * n_tok, (j + 1) * n_tok)
        o = (gates[:, 3 * hd:3 * hd + 1] * o_c[rows] + gates[:, 3 * hd + 1:3 * hd + 2] * o_s[rows]
             + gates[:, 3 * hd + 2:3 * hd + 3] * o_w[rows])
        o_ref[:, hd * HEAD_DIM:(hd + 1) * HEAD_DIM] = o.astype(o_ref.dtype)


def _attn_prompt_kernel(q_ref, ks_ref, kw_ref, kvc_ref, gate_ref, ovt_ref, o_ref):
    b = pl.program_id(0)

    @pl.when(b == BATCH)
    def _():
        o_ref[...] = jnp.zeros_like(o_ref)

    @pl.when(b < BATCH)
    def _():
        _attn_prompt_block(q_ref, ks_ref, kw_ref, kvc_ref, gate_ref, ovt_ref, o_ref)


def _attn_prompt_block(q_ref, ks_ref, kw_ref, kvc_ref, gate_ref, ovt_ref, o_ref):
    qb = pl.program_id(1)
    q0 = qb * Q_BLOCK
    qpos_col = q0 + lax.broadcasted_iota(jnp.int32, (Q_BLOCK, 1), 0)
    qpos_row = q0 + lax.broadcasted_iota(jnp.int32, (1, Q_BLOCK), 1)
    gates = gate_ref[...]
    ovt = ovt_ref[...]
    q_end = q0 + Q_BLOCK - 1

    def key_pos(key0, tile):
        return key0 + lax.broadcasted_iota(jnp.int32, (Q_BLOCK, tile), 1)

    def win_mask(key0):
        dist = qpos_col - key_pos(key0, WIN_TILE)
        return (dist >= 0) & (dist <= WINDOW)

    for g0 in range(0, N_KV_HEADS, FLASH_GROUPS):
        gs = range(g0, g0 + FLASH_GROUPS)
        cols = [g * HEAD_DIM for g in gs]
        qgs, o_cs, sel_masks = [], [], []
        for g, col in zip(gs, cols):
            qg = jnp.concatenate([q_ref[:, (g * GROUP + j) * HEAD_DIM:(g * GROUP + j + 1) * HEAD_DIM]
                                  for j in range(GROUP)], axis=0)
            vcol = N_KV_HEADS * HEAD_DIM + col
            o_c, sel = _cmp_and_select(qg, kvc_ref[0, :, col:col + HEAD_DIM], kvc_ref[0, :, vcol:vcol + HEAD_DIM],
                                       ovt, qpos_col, qpos_row, Q_BLOCK)

            def sel_mask(key0, sel=sel):
                chosen = jnp.dot(sel, _block_expand(SEL_TILE, key0), preferred_element_type=F32) > 0.5
                return chosen & (key_pos(key0, SEL_TILE) <= qpos_col)

            qgs.append(qg)
            o_cs.append(o_c)
            sel_masks.append(sel_mask)
        o_ss = _flash(qgs, ks_ref, ks_ref, cols, 0, q_end // SEL_TILE + 1, sel_masks, SEL_TILE)
        o_ws = _flash(qgs, kw_ref, kw_ref, cols, jnp.maximum(q0 - WINDOW, 0) // WIN_TILE, q_end // WIN_TILE + 1,
                      [win_mask] * FLASH_GROUPS, WIN_TILE)
        for g, o_c, o_s, o_w in zip(gs, o_cs, o_ss, o_ws):
            _combine_heads(g, Q_BLOCK, o_c, o_s, o_w, gates, o_ref)


def _attn_prompt(q, kvb_sel, kvb_win, kvc, gates, ovt):
    nqb = SEQ // Q_BLOCK
    n_fill = T_S // Q_BLOCK
    last = BATCH - 1
    qrow = lambda b, i: (jnp.minimum(b, last) * nqb + i, 0)
    orow = lambda b, i: (jnp.where(b < BATCH, b * nqb + i, BATCH * nqb + jnp.minimum(i, n_fill - 1)), 0)
    return pl.pallas_call(
        _attn_prompt_kernel,
        grid=(BATCH + 1, nqb),
        in_specs=[pl.BlockSpec((Q_BLOCK, N_Q_COLS), qrow),
                  pl.BlockSpec((SEQ, KV_COLS), lambda b, i: (jnp.minimum(b, last), 0)),
                  pl.BlockSpec((SEQ, KV_COLS), lambda b, i: (jnp.minimum(b, last), 0)),
                  pl.BlockSpec((1, N_CMP, KV_COLS), lambda b, i: (jnp.minimum(b, last), 0, 0)),
                  pl.BlockSpec((Q_BLOCK, 128), qrow),
                  pl.BlockSpec(ovt.shape, lambda b, i: (0, 0))],
        out_specs=pl.BlockSpec((Q_BLOCK, N_Q_COLS), orow),
        out_shape=jax.ShapeDtypeStruct((T_ALL, N_Q_COLS), BF16),
        compiler_params=_params(("arbitrary", "arbitrary")),
        name="attn_prompt",
    )(q, kvb_sel, kvb_win, kvc, gates, ovt)


TAIL = 128
SEL_KEYS = PAST_LEN + TAIL
WIN_KEYS = WINDOW + TAIL
WIN_ROWS = WINDOW * ROW_SPLIT
NEW_ROWS = DEC_SEQ * ROW_SPLIT


def _attn_sample_kernel(pt_ref, *refs):
    pages = refs[:N_PAGES]
    q_ref, news_ref, neww_ref, win_ref, kvc_ref, gate_ref, ovt_ref, o_ref, nwin_ref, sbuf, wbuf = refs[N_PAGES:]
    def by_group(ref):
        x = ref[...]
        return pltpu.einshape("tcd->ctd", x.reshape(x.shape[0] // ROW_SPLIT, ROW_SPLIT, HEAD_DIM))

    pad = jnp.zeros((ROW_SPLIT, TAIL - DEC_SEQ, HEAD_DIM), F32)
    for p in range(N_PAGES):
        sbuf[:, p * PAGE_SIZE:(p + 1) * PAGE_SIZE, :] = by_group(pages[p]).astype(BF16)
    sbuf[:, PAST_LEN:, :] = jnp.concatenate([by_group(news_ref), pad], axis=1).astype(BF16)
    wbuf[:, :WINDOW, :] = by_group(win_ref).astype(BF16)
    wbuf[:, WINDOW:, :] = jnp.concatenate([by_group(neww_ref), pad], axis=1).astype(BF16)
    nwin_ref[:WIN_ROWS - NEW_ROWS, :] = win_ref[NEW_ROWS:, :]
    nwin_ref[WIN_ROWS - NEW_ROWS:, :] = neww_ref[...]

    tcol = lax.broadcasted_iota(jnp.int32, (DEC_SEQ, 1), 0)
    qpos_col = PAST_LEN + tcol
    qpos_row = PAST_LEN + jnp.minimum(lax.broadcasted_iota(jnp.int32, (1, 128), 1), DEC_SEQ - 1)
    gates = gate_ref[...]
    ovt = ovt_ref[...]
    expand = _block_expand(SEL_KEYS, 0)
    sel_pos = lax.broadcasted_iota(jnp.int32, (DEC_SEQ, SEL_KEYS), 1)
    win_c = lax.broadcasted_iota(jnp.int32, (DEC_SEQ, WIN_KEYS), 1)
    win_mask = _rep4((win_c >= tcol) & (win_c <= WINDOW + tcol))
    for g in range(N_KV_HEADS):
        col = g * HEAD_DIM
        vcol = N_KV_HEADS * HEAD_DIM + col
        qg = jnp.concatenate([q_ref[:, (g * GROUP + j) * HEAD_DIM:(g * GROUP + j + 1) * HEAD_DIM]
                              for j in range(GROUP)], axis=0).astype(BF16)
        o_c, sel = _cmp_and_select(qg, kvc_ref[0, :, col:col + HEAD_DIM], kvc_ref[0, :, vcol:vcol + HEAD_DIM],
                                   ovt, qpos_col, qpos_row, DEC_SEQ)
        chosen = jnp.dot(sel, expand, preferred_element_type=F32)[:DEC_SEQ] > 0.5
        mask = _rep4(chosen & (sel_pos <= qpos_col))
        s = lax.dot_general(qg, sbuf[g], _NT, preferred_element_type=F32)
        e, l = _masked_exp(s, mask)
        o_s = jnp.dot(e.astype(BF16), sbuf[N_KV_HEADS + g], preferred_element_type=F32)
        o_s = o_s / jnp.maximum(l, 1e-30)
        s = lax.dot_general(qg, wbuf[g], _NT, preferred_element_type=F32)
        e, l = _masked_exp(s, win_mask)
        o_w = jnp.dot(e.astype(BF16), wbuf[N_KV_HEADS + g], preferred_element_type=F32)
        o_w = o_w / jnp.maximum(l, 1e-30)
        _combine_heads(g, DEC_SEQ, o_c, o_s, o_w, gates, o_ref)


def _attn_sample(q_s, new_sel, new_win, cache_sel, state_win, page_table, kvc, gates, ovt):
    row0 = T_P // DEC_SEQ
    grid_spec = pltpu.PrefetchScalarGridSpec(
        num_scalar_prefetch=1,
        grid=(DEC_BATCH,),
        in_specs=[*_page_specs(),
                  pl.BlockSpec((DEC_SEQ, N_Q_COLS), lambda s, pt: (s, 0)),
                  pl.BlockSpec((NEW_ROWS, HEAD_DIM), lambda s, pt: (s, 0)),
                  pl.BlockSpec((NEW_ROWS, HEAD_DIM), lambda s, pt: (s, 0)),
                  pl.BlockSpec((WIN_ROWS, HEAD_DIM), lambda s, pt: (s, 0)),
                  pl.BlockSpec((1, N_CMP, KV_COLS), lambda s, pt: (s, 0, 0)),
                  pl.BlockSpec((DEC_SEQ, 128), lambda s, pt: (row0 + s, 0)),
                  pl.BlockSpec(ovt.shape, lambda s, pt: (0, 0))],
        out_specs=[pl.BlockSpec((DEC_SEQ, N_Q_COLS), lambda s, pt: (s, 0)),
                   pl.BlockSpec((WIN_ROWS, HEAD_DIM), lambda s, pt: (s, 0))],
        scratch_shapes=[pltpu.VMEM((ROW_SPLIT, SEL_KEYS, HEAD_DIM), BF16),
                        pltpu.VMEM((ROW_SPLIT, WIN_KEYS, HEAD_DIM), BF16)],
    )
    return pl.pallas_call(
        _attn_sample_kernel,
        grid_spec=grid_spec,
        out_shape=[jax.ShapeDtypeStruct((T_S, N_Q_COLS), F32),
                   jax.ShapeDtypeStruct((DEC_BATCH * WIN_ROWS, HEAD_DIM), F32)],
        compiler_params=_params(("arbitrary",)),
        name="attn_sample",
    )(page_table, *([cache_sel] * N_PAGES), q_s, new_sel, new_win, state_win, kvc, gates, ovt)


def _overlap_t(ns, nsp):
    c0 = jnp.arange(N_CMP)[None, :] * CMP_STRIDE
    s0 = jnp.arange(nsp)[:, None] * SEL_BLOCK
    ov = jnp.minimum(c0 + CMP_BLOCK, s0 + SEL_BLOCK) - jnp.maximum(c0, s0)
    ov = jnp.maximum(ov, 0).astype(F32) / CMP_BLOCK
    ov = jnp.where((jnp.arange(nsp)[:, None] < ns) & (jnp.arange(N_CMP)[None, :] < N_CMP - 1), ov, 0.0)
    return ov.astype(BF16)


def _outproj_kernel(npt, o_ref, w_ref, xp_ref, xs_ref, gp, gs, shp, shs, scp, scs, lng, lnb, x1_ref, h1_ref):
    i = pl.program_id(0)
    y = jnp.dot(o_ref[...], w_ref[...], preferred_element_type=F32)
    x = _pick(i, npt, xp_ref, xs_ref)
    x1 = _layer_norm(ALPHA * x + _pick(i, npt, gp, gs) * y, lng[...], lnb[...])
    x1_ref[...] = x1
    h1_ref[...] = (x1 * (1.0 + _pick(i, npt, scp, scs)) + _pick(i, npt, shp, shs)).astype(BF16)


def _outproj(o, w_out_bf, xp, xs, mp, ms, lng, lnb, tm=256):
    row = pl.BlockSpec((tm, D_MODEL), lambda i: (i, 0))
    vec = pl.BlockSpec((1, D_MODEL), lambda i: (0, 0))
    return pl.pallas_call(
        functools.partial(_outproj_kernel, T_P // tm),
        grid=(T_ALL // tm,),
        in_specs=[row, pl.BlockSpec((N_Q_COLS, D_MODEL), lambda i: (0, 0)), *_x_specs(tm),
                  *_mod_specs(2, tm), *_mod_specs(3, tm), *_mod_specs(4, tm), vec, vec],
        out_specs=[row, row],
        out_shape=[jax.ShapeDtypeStruct((T_ALL, D_MODEL), F32), jax.ShapeDtypeStruct((T_ALL, D_MODEL), BF16)],
        compiler_params=_params(("parallel",)),
        name="nsa_out_postnorm",
    )(o, w_out_bf, xp, xs, mp, ms, mp, ms, mp, ms, lng, lnb)


def _swiglu_up_kernel(x_ref, wg_ref, wu_ref, a_ref):
    x = x_ref[...]
    g = jnp.dot(x, wg_ref[...], preferred_element_type=F32)
    u = jnp.dot(x, wu_ref[...], preferred_element_type=F32)
    a_ref[...] = (g * jax.nn.sigmoid(g) * u).astype(BF16)


def _ffn_up(h, w_up_bf, tm=1024, tn=512):
    nj = D_FF // tn
    return pl.pallas_call(
        _swiglu_up_kernel,
        grid=(T_ALL // tm, nj),
        in_specs=[pl.BlockSpec((tm, D_MODEL), lambda i, j: (i, 0)),
                  pl.BlockSpec((D_MODEL, tn), lambda i, j: (0, j)),
                  pl.BlockSpec((D_MODEL, tn), lambda i, j: (0, nj + j))],
        out_specs=pl.BlockSpec((tm, tn), lambda i, j: (i, j)),
        out_shape=jax.ShapeDtypeStruct((T_ALL, D_FF), BF16),
        compiler_params=_params(("parallel", "arbitrary")),
        name="ffn_up",
    )(h, w_up_bf, w_up_bf)


def _ffn_down_kernel(npt, a_ref, w_ref, x_ref, gp, gs, lng, lnb, x2_ref, acc):
    i = pl.program_id(0)
    k = pl.program_id(1)

    @pl.when(k == 0)
    def _():
        acc[...] = jnp.zeros_like(acc)

    acc[...] += jnp.dot(a_ref[...], w_ref[...], preferred_element_type=F32)

    @pl.when(k == pl.num_programs(1) - 1)
    def _():
        x2_ref[...] = _layer_norm(ALPHA * x_ref[...] + _pick(i, npt, gp, gs) * acc[...], lng[...], lnb[...])


def _ffn_down(a, w_down_bf, x, mp, ms, lng, lnb, tm=512, tk=1408):
    row = pl.BlockSpec((tm, D_MODEL), lambda i, k: (i, 0))
    vec = pl.BlockSpec((1, D_MODEL), lambda i, k: (0, 0))
    return pl.pallas_call(
        functools.partial(_ffn_down_kernel, T_P // tm),
        grid=(T_ALL // tm, D_FF // tk),
        in_specs=[pl.BlockSpec((tm, tk), lambda i, k: (i, k)),
                  pl.BlockSpec((tk, D_MODEL), lambda i, k: (k, 0)),
                  row, *_mod_specs(5, tm), vec, vec],
        out_specs=row,
        out_shape=jax.ShapeDtypeStruct((T_ALL, D_MODEL), F32),
        scratch_shapes=[pltpu.VMEM((tm, D_MODEL), F32)],
        compiler_params=_params(("parallel", "arbitrary")),
        name="ffn_down_postnorm",
    )(a, w_down_bf, x, mp, ms, lng, lnb)


HALO = 16


def _pool_kernel(prompt, tiles_per_seq, n_tiles, *refs):
    i = pl.program_id(0)
    x1_ref, h1_ref, _, lg_ref = refs[-4:]

    @pl.when(i >= n_tiles)
    def _():
        x1_ref[...] = jnp.zeros_like(x1_ref)
        h1_ref[...] = jnp.zeros_like(h1_ref)
        lg_ref[...] = jnp.zeros_like(lg_ref)

    @pl.when(i < n_tiles)
    def _():
        _pool_tile(prompt, tiles_per_seq, *refs)


def _pool_tile(prompt, tiles_per_seq, x_ref, halo_ref, sh_m, sc_m, g_m, sh_f, sc_f, hsh, hsc,
               wp_ref, ps_ref, lng, lnb, wr_hi, wr_lo, br_ref, x1_ref, h1_ref, hm_ref, lg_ref):
    i = pl.program_id(0)
    tm = x_ref.shape[0]
    x = x_ref[...]
    h = x * (1.0 + sc_m[...]) + sh_m[...]
    hm_ref[...] = h
    if prompt:
        hh = halo_ref[...] * (1.0 + hsc[...]) + hsh[...]
        hh = jnp.where(i % tiles_per_seq == 0, 0.0, hh)
        pos = (i % tiles_per_seq) * tm + lax.broadcasted_iota(jnp.int32, (tm, 1), 0)
    else:
        hh = jnp.zeros((HALO, D_MODEL), F32)
    ext = jnp.concatenate([hh, h], axis=0)
    ys = []
    for gi, w in enumerate(POOL_WINDOWS):
        a = ext[:, gi * POOL_GROUP:(gi + 1) * POOL_GROUP]
        s = a
        d = 1
        while d < w:
            s = s + pltpu.roll(s, d, axis=0)
            d *= 2
        s = s[HALO:]
        if prompt:
            inv = 1.0 / jnp.minimum(pos + 1, w).astype(F32)
        else:
            inv = 1.0 / w
        diff = (s * inv - a[HALO:]).astype(BF16)
        ys.append(jnp.dot(diff, wp_ref[gi], preferred_element_type=F32))
    y = jnp.concatenate(ys, axis=1) * ps_ref[...]
    x1 = _layer_norm(ALPHA * x + g_m[...] * y, lng[...], lnb[...])
    x1_ref[...] = x1
    h1 = x1 * (1.0 + sc_f[...]) + sh_f[...]
    h1_ref[...] = h1
    hi, lo = _split_bf16(h1)
    lg_ref[...] = (jnp.dot(hi, wr_hi[...], preferred_element_type=F32)
                   + jnp.dot(hi, wr_lo[...], preferred_element_type=F32)
                   + jnp.dot(lo, wr_hi[...], preferred_element_type=F32) + br_ref[...])


def _pool_call(prompt, x, n, n_out, hm_spec, hm_rows, mods, w_pool_bf, pool_scale, lng, lnb, wr_hi, wr_lo,
               b_router, tm):
    row = pl.BlockSpec((tm, D_MODEL), lambda i: (i, 0))
    vec = pl.BlockSpec((1, D_MODEL), lambda i: (0, 0))
    halo = pl.BlockSpec((HALO, D_MODEL), lambda i: (jnp.maximum(i * (tm // HALO) - 1, 0), 0))
    specs = [s for s, _ in mods]
    arrs = [a for _, a in mods]
    return pl.pallas_call(
        functools.partial(_pool_kernel, prompt, SEQ // tm, n // tm),
        grid=(n_out // tm,),
        in_specs=[row, halo, *specs,
                  pl.BlockSpec((4, POOL_GROUP, POOL_GROUP), lambda i: (0, 0, 0)), vec, vec, vec,
                  pl.BlockSpec((D_MODEL, 128), lambda i: (0, 0)), pl.BlockSpec((D_MODEL, 128), lambda i: (0, 0)),
                  pl.BlockSpec((1, 128), lambda i: (0, 0))],
        out_specs=[row, row, hm_spec, pl.BlockSpec((tm, 128), lambda i: (i, 0))],
        out_shape=[jax.ShapeDtypeStruct((n_out, D_MODEL), F32), jax.ShapeDtypeStruct((n_out, D_MODEL), F32),
                   jax.ShapeDtypeStruct((hm_rows, D_MODEL), F32), jax.ShapeDtypeStruct((n_out, 128), F32)],
        compiler_params=_params(("arbitrary",)),
        name="pool_prompt" if prompt else "pool_sample",
    )(x, x, *arrs, w_pool_bf, pool_scale, lng, lnb, wr_hi, wr_lo, b_router)


_POOL_MOD_CHUNKS = (0, 1, 2, 3, 4, 0, 1)


def _pool_prompt(x, mp, *rest, tm=256):
    tpb = SEQ // tm
    seq = lambda i: jnp.minimum(i // tpb, BATCH - 1)
    mods = [(pl.BlockSpec((None, 1, D_MODEL), functools.partial(lambda k, i: (seq(i), 0, k), k)), mp)
            for k in _POOL_MOD_CHUNKS]
    hm_spec = pl.BlockSpec((tm, D_MODEL), lambda i: (seq(i), 0))
    return _pool_call(True, x, T_P, T_ALL, hm_spec, BATCH * tm, mods, *rest, tm)


def _pool_sample(x_ext, ms_ext, *rest, tm=8 * POOL_EXT):
    n = DEC_BATCH * POOL_EXT
    mods = [(pl.BlockSpec((tm, D_MODEL), functools.partial(lambda k, i: (i, k), k)), ms_ext)
            for k in _POOL_MOD_CHUNKS[:5]]
    mods += [(pl.BlockSpec((HALO, D_MODEL), functools.partial(lambda k, i: (0, k), k)), ms_ext)
             for k in _POOL_MOD_CHUNKS[5:]]
    hm_spec = pl.BlockSpec((tm, D_MODEL), lambda i: (i, 0))
    return _pool_call(False, x_ext, n, n, hm_spec, n, mods, *rest, tm)


N_MOE_BLOCKS = -(-(T_ALL * TOP_K) // MOE_BLOCK) + N_EXPERTS
N_SLOTS = N_MOE_BLOCKS * MOE_BLOCK


def _row_gather_kernel(idx_ref, src_hbm, o_ref, buf, sem):
    i = pl.program_id(0)
    nb = pl.num_programs(0)
    bsz = o_ref.shape[0]

    def issue(blk, slot):
        def body(r, carry):
            row = idx_ref[blk * bsz + r]
            pltpu.make_async_copy(src_hbm.at[pl.ds(row, 1)], buf.at[slot, pl.ds(r, 1)], sem.at[slot]).start()
            return carry
        lax.fori_loop(0, bsz, body, 0, unroll=8)

    @pl.when(i == 0)
    def _():
        issue(0, 0)

    @pl.when(i + 1 < nb)
    def _():
        issue(i + 1, (i + 1) % 2)

    slot = i % 2
    pltpu.make_async_copy(src_hbm.at[pl.ds(0, bsz)], buf.at[slot], sem.at[slot]).wait()
    o_ref[...] = buf[slot].astype(BF16)


def _gather_rows_bf16(src, idx):
    grid_spec = pltpu.PrefetchScalarGridSpec(
        num_scalar_prefetch=1,
        grid=(N_MOE_BLOCKS,),
        in_specs=[pl.BlockSpec(memory_space=pl.ANY)],
        out_specs=pl.BlockSpec((MOE_BLOCK, D_MODEL), lambda i, idx: (i, 0)),
        scratch_shapes=[pltpu.VMEM((2, MOE_BLOCK, D_MODEL), F32), pltpu.SemaphoreType.DMA((2,))],
    )
    return pl.pallas_call(
        _row_gather_kernel,
        grid_spec=grid_spec,
        out_shape=jax.ShapeDtypeStruct((N_SLOTS, D_MODEL), BF16),
        compiler_params=_params(("arbitrary",)),
        name="moe_dispatch_gather",
    )(idx, src)


def _moe_up_kernel(be_ref, fresh_ref, live_ref, x_ref, wg_ref, wu_ref, a_ref, wg_s, wu_s):
    i = pl.program_id(1)

    @pl.when(fresh_ref[i] == 1)
    def _():
        wg_s[...] = wg_ref[...].astype(BF16)
        wu_s[...] = wu_ref[...].astype(BF16)

    @pl.when(live_ref[i] == 1)
    def _():
        x = x_ref[...]
        g = jnp.dot(x, wg_s[...], preferred_element_type=F32)
        u = jnp.dot(x, wu_s[...], preferred_element_type=F32)
        a_ref[...] = (g * jax.nn.sigmoid(g) * u).astype(BF16)

    @pl.when(live_ref[i] == 0)
    def _():
        a_ref[...] = jnp.zeros_like(a_ref)


def _moe_up(xb, w_up, block_expert, fresh, live, tn=1024):
    nj = D_FF_EXPERT // tn
    grid_spec = pltpu.PrefetchScalarGridSpec(
        num_scalar_prefetch=3,
        grid=(nj, N_MOE_BLOCKS),
        in_specs=[pl.BlockSpec((MOE_BLOCK, D_MODEL), lambda j, i, be, fr, lv: (i, 0)),
                  pl.BlockSpec((None, D_MODEL, tn), lambda j, i, be, fr, lv: (be[i], 0, j)),
                  pl.BlockSpec((None, D_MODEL, tn), lambda j, i, be, fr, lv: (be[i], 0, nj + j))],
        out_specs=pl.BlockSpec((MOE_BLOCK, tn), lambda j, i, be, fr, lv: (i, j)),
        scratch_shapes=[pltpu.VMEM((D_MODEL, tn), BF16), pltpu.VMEM((D_MODEL, tn), BF16)],
    )
    return pl.pallas_call(
        _moe_up_kernel,
        grid_spec=grid_spec,
        out_shape=jax.ShapeDtypeStruct((N_SLOTS, D_FF_EXPERT), BF16),
        compiler_params=_params(("arbitrary", "arbitrary")),
        name="moe_up",
    )(block_expert, fresh, live, xb, w_up, w_up)


def _moe_down_kernel(be_ref, fresh_ref, live_ref, a_ref, w_ref, sg_ref, y_ref, w_s):
    i = pl.program_id(1)

    @pl.when(fresh_ref[i] == 1)
    def _():
        w_s[...] = w_ref[...].astype(BF16)

    @pl.when(live_ref[i] == 1)
    def _():
        y_ref[...] = jnp.dot(a_ref[...], w_s[...], preferred_element_type=F32) * sg_ref[...]

    @pl.when(live_ref[i] == 0)
    def _():
        y_ref[...] = jnp.zeros_like(y_ref)


def _moe_down(act, w_down, slot_gate, block_expert, fresh, live, tn=512):
    grid_spec = pltpu.PrefetchScalarGridSpec(
        num_scalar_prefetch=3,
        grid=(D_MODEL // tn, N_MOE_BLOCKS),
        in_specs=[pl.BlockSpec((MOE_BLOCK, D_FF_EXPERT), lambda j, i, be, fr, lv: (i, 0)),
                  pl.BlockSpec((None, D_FF_EXPERT, tn), lambda j, i, be, fr, lv: (be[i], 0, j)),
                  pl.BlockSpec((MOE_BLOCK, 1), lambda j, i, be, fr, lv: (i, 0))],
        out_specs=pl.BlockSpec((MOE_BLOCK, tn), lambda j, i, be, fr, lv: (i, j)),
        scratch_shapes=[pltpu.VMEM((D_FF_EXPERT, tn), BF16)],
    )
    return pl.pallas_call(
        _moe_down_kernel,
        grid_spec=grid_spec,
        out_shape=jax.ShapeDtypeStruct((N_SLOTS, D_MODEL), F32),
        compiler_params=_params(("arbitrary", "arbitrary")),
        name="moe_down",
    )(block_expert, fresh, live, act, w_down, slot_gate)


def _route(logits):
    n = logits.shape[0]
    top_logit, top_e = lax.top_k(logits, TOP_K)
    gate = jax.nn.softmax(top_logit, axis=-1)
    e_flat = top_e.reshape(-1)
    order = jnp.argsort(e_flat, stable=True)
    e_sorted = e_flat[order]
    counts = jnp.bincount(e_flat, length=N_EXPERTS)
    padded = (counts + MOE_BLOCK - 1) // MOE_BLOCK * MOE_BLOCK
    start = jnp.cumsum(counts) - counts
    pend = jnp.cumsum(padded)
    pstart = pend - padded
    dest = (pstart[e_sorted] + jnp.arange(n * TOP_K) - start[e_sorted]).astype(jnp.int32)
    blk0 = jnp.arange(N_MOE_BLOCKS) * MOE_BLOCK
    block_expert = jnp.minimum(jnp.searchsorted(pend, blk0, side='right'), N_EXPERTS - 1).astype(jnp.int32)
    live = (blk0 < pend[-1]).astype(jnp.int32)
    fresh = jnp.concatenate([jnp.ones((1,), jnp.int32),
                             (block_expert[1:] != block_expert[:-1]).astype(jnp.int32)])
    slot = jnp.arange(N_SLOTS)
    se = jnp.repeat(block_expert, MOE_BLOCK)
    off = slot - pstart[se]
    valid = off < counts[se]
    src = jnp.clip(start[se] + off, 0, n * TOP_K - 1)
    flat = order[src]
    slot_tok = jnp.where(valid, flat // TOP_K, n - 1).astype(jnp.int32)
    slot_gate = jnp.where(valid, gate.reshape(-1)[flat], 0.0)
    slot_of = dest[jnp.argsort(order)].reshape(n, TOP_K)
    return slot_tok, slot_gate, block_expert, fresh, live, slot_of


def _final_kernel(npt, sa_ref, sb_ref, x_ref, y_hbm, gp, gs, lng, lnb, op_ref, os_ref, buf, sem):
    i = pl.program_id(0)
    nb = pl.num_programs(0)
    tm = x_ref.shape[0]

    def issue(blk, slot):
        def body(r, carry):
            t = blk * tm + r
            pltpu.make_async_copy(y_hbm.at[pl.ds(sa_ref[t], 1)], buf.at[slot, 0, pl.ds(r, 1)], sem.at[slot]).start()
            pltpu.make_async_copy(y_hbm.at[pl.ds(sb_ref[t], 1)], buf.at[slot, 1, pl.ds(r, 1)], sem.at[slot]).start()
            return carry
        lax.fori_loop(0, tm, body, 0, unroll=8)

    @pl.when(i == 0)
    def _():
        issue(0, 0)

    @pl.when(i + 1 < nb)
    def _():
        issue(i + 1, (i + 1) % 2)

    slot = i % 2
    for half in range(TOP_K):
        pltpu.make_async_copy(y_hbm.at[pl.ds(0, tm)], buf.at[slot, half], sem.at[slot]).wait()
    y = buf[slot, 0] + buf[slot, 1]
    out = _layer_norm(ALPHA * x_ref[...] + _pick(i, npt, gp, gs) * y, lng[...], lnb[...])

    @pl.when(i < npt)
    def _():
        op_ref[...] = out

    @pl.when(i >= npt)
    def _():
        os_ref[...] = out


def _final(x, yb, slot_a, slot_b, mp, ms, lng, lnb, tm=256):
    npt = T_P // tm
    row = pl.BlockSpec((tm, D_MODEL), lambda i, *_: (i, 0))
    vec = pl.BlockSpec((1, D_MODEL), lambda i, *_: (0, 0))
    grid_spec = pltpu.PrefetchScalarGridSpec(
        num_scalar_prefetch=2,
        grid=(T_ALL // tm,),
        in_specs=[row, pl.BlockSpec(memory_space=pl.ANY), *_mod_specs(5, tm), vec, vec],
        out_specs=[pl.BlockSpec((tm, D_MODEL), lambda i, *_: (jnp.minimum(i, npt - 1), 0)),
                   pl.BlockSpec((tm, D_MODEL), lambda i, *_: (jnp.maximum(i - npt, 0), 0))],
        scratch_shapes=[pltpu.VMEM((2, 2, tm, D_MODEL), F32), pltpu.SemaphoreType.DMA((2,))],
    )
    return pl.pallas_call(
        functools.partial(_final_kernel, npt),
        grid_spec=grid_spec,
        out_shape=[jax.ShapeDtypeStruct((T_P, D_MODEL), F32), jax.ShapeDtypeStruct((T_S, D_MODEL), F32)],
        compiler_params=_params(("arbitrary",)),
        name="moe_combine_postnorm",
    )(slot_a, slot_b, x, yb, mp, ms, lng, lnb)


def kernel(x_prompt, x_sample, cache_cmp_kv, cache_sel_kv, state_win_kv, state_pool, page_table,
           c_prompt, c_sample, w_ada, b_ada, ln_gain, ln_bias, w_nsa_in, w_cmp1, w_cmp2, cmp_pos,
           w_nsa_out, w_pool, pool_scale, w_ffn_up, w_ffn_down, w_router, b_router, w_moe_up, w_moe_down):
    xp = x_prompt.reshape(T_P, D_MODEL)
    xs = x_sample.reshape(T_S, D_MODEL)
    mods = _ada(jnp.concatenate([c_prompt, c_sample], axis=0), w_ada, b_ada)

    def layer_mods(i):
        return mods[i, :BATCH].reshape(BATCH, 1, 6 * D_MODEL), jnp.repeat(mods[i, BATCH:], DEC_SEQ, axis=0)

    mp, ms = layer_mods(0)
    lng = ln_gain[0].reshape(2, 1, D_MODEL)
    lnb = ln_bias[0].reshape(2, 1, D_MODEL)
    h = _modulate(xp, xs, mp, ms, 0, 1)
    w_in_bf = w_nsa_in[0].astype(BF16)
    w_gate_bf = jnp.pad(w_in_bf[:, N_Q_COLS + 3 * KV_COLS:], ((0, 0), (0, 128 - N_GATE_COLS)))
    q, (cmp_kv, sel_kv, win_kv), gates = _nsa_project(h, w_in_bf, w_gate_bf)

    w1 = w_cmp1[0].astype(BF16)
    wc = w1.reshape(2, 2, CMP_STRIDE, HEAD_DIM, HEAD_DIM).transpose(0, 2, 3, 1, 4).reshape(
        2, CMP_STRIDE * HEAD_DIM, 2 * HEAD_DIM)
    cpos = jnp.pad(cmp_pos[0].reshape(2, 1, CMP_BLOCK * HEAD_DIM), ((0, 0), (0, 7), (0, 0)))
    cw = (wc, w1, w_cmp2[0].astype(BF16), cpos)
    stored = lambda a: a.reshape(-1, HEAD_DIM)
    kvc_p = _compress_prompt(cmp_kv[0], cw)
    kvc_s = _compress_sample(stored(cache_cmp_kv), page_table, cw)

    o = _attn_prompt(q, sel_kv[2], win_kv[2], kvc_p, gates, _overlap_t(SEQ // SEL_BLOCK, 32))
    ns_s = -(-(PAST_LEN + DEC_SEQ) // SEL_BLOCK)
    o_s, new_win_rows = _attn_sample(q[T_P:].astype(F32), sel_kv[1], win_kv[1], stored(cache_sel_kv),
                                     stored(state_win_kv), page_table, kvc_s, gates, _overlap_t(ns_s, 40))
    o = lax.dynamic_update_slice(o, o_s.astype(BF16), (T_P, 0))
    x1, h1 = _outproj(o, w_nsa_out[0].astype(BF16), xp, xs, mp, ms, lng[0], lnb[0])
    act = _ffn_up(h1, w_ffn_up[0].astype(BF16))
    x2 = _ffn_down(act, w_ffn_down[0].astype(BF16), x1, mp, ms, lng[1], lnb[1])

    kv_p = lambda a: a.reshape(1, BATCH, SEQ, 2, N_KV_HEADS, HEAD_DIM)
    kv_s = lambda a: a.reshape(1, DEC_BATCH, -1, 2, N_KV_HEADS, HEAD_DIM)
    new_cmp_p, new_cmp_s = kv_p(cmp_kv[0]), kv_s(cmp_kv[1])
    new_sel_p, new_sel_s = kv_p(sel_kv[0]), kv_s(sel_kv[1])
    new_win_p = kv_p(win_kv[0])[:, :, SEQ - WINDOW:]
    new_win_s = kv_s(new_win_rows)

    mp, ms = layer_mods(1)
    lng = ln_gain[1].reshape(2, 1, D_MODEL)
    lnb = ln_bias[1].reshape(2, 1, D_MODEL)
    wr_pad = jnp.pad(w_router[0], ((0, 0), (0, 128 - N_EXPERTS)))
    wr_hi, wr_lo = _split_bf16(wr_pad)
    br = jnp.pad(b_router[0].reshape(1, N_EXPERTS), ((0, 0), (0, 128 - N_EXPERTS)))
    pool_args = (w_pool[0].astype(BF16), pool_scale[0].reshape(1, D_MODEL), lng[0], lnb[0], wr_hi, wr_lo, br)
    x3, h3, hmp, lg = _pool_prompt(x2, mp, *pool_args)

    ms3 = ms.reshape(DEC_BATCH, DEC_SEQ, 6 * D_MODEL)
    ms_ext = jnp.pad(ms3, ((0, 0), (POOL_EXT - DEC_SEQ, 0), (0, 0))).reshape(DEC_BATCH * POOL_EXT, 6 * D_MODEL)
    xs_ext = jnp.concatenate([jnp.zeros((DEC_BATCH, 1, D_MODEL), F32), state_pool[0],
                              x2[T_P:].reshape(DEC_BATCH, DEC_SEQ, D_MODEL)], axis=1)
    xs_ext = xs_ext.reshape(DEC_BATCH * POOL_EXT, D_MODEL)
    x1s, h1s, hms, lgs = _pool_sample(xs_ext, ms_ext, *pool_args)
    new_rows = lambda a: a.reshape(DEC_BATCH, POOL_EXT, -1)[:, POOL_EXT - DEC_SEQ:].reshape(T_S, -1)
    x3 = lax.dynamic_update_slice(x3, new_rows(x1s), (T_P, 0))
    h3 = lax.dynamic_update_slice(h3, new_rows(h1s), (T_P, 0))
    logits = lax.dynamic_update_slice(lg, new_rows(lgs), (T_P, 0))[:, :N_EXPERTS]
    new_pool_p = hmp.reshape(BATCH, -1, D_MODEL)[None, :, -POOL_BUF:]
    new_pool_s = hms.reshape(DEC_BATCH, POOL_EXT, D_MODEL)[None, :, POOL_EXT - POOL_BUF:]

    slot_tok, slot_gate, block_expert, fresh, live, slot_of = _route(logits)
    xb = _gather_rows_bf16(h3, slot_tok)
    act = _moe_up(xb, w_moe_up[0], block_expert, fresh, live)
    yb = _moe_down(act, w_moe_down[0], slot_gate.reshape(-1, 1), block_expert, fresh, live)
    y_prompt, y_sample = _final(x3, yb, slot_of[:, 0], slot_of[:, 1], mp, ms, lng[1], lnb[1])

    return (y_prompt.reshape(BATCH, SEQ, D_MODEL), y_sample.reshape(DEC_BATCH, DEC_SEQ, D_MODEL),
            new_cmp_p, new_cmp_s, new_sel_p, new_sel_s, new_win_p, new_win_s, new_pool_p, new_pool_s)
```

```python
import functools

import jax
import jax.numpy as jnp
from jax import lax
from jax.experimental import pallas as pl
from jax.experimental.pallas import tpu as pltpu

F32 = jnp.float32
BF16 = jnp.bfloat16

D_MODEL = 2048
BATCH = 8
SEQ = 2048
DEPTH = 2
DEC_BATCH = 128
DEC_SEQ = 8
PAST_LEN = 2048
PAGE_SIZE = 128
N_PAGES = PAST_LEN // PAGE_SIZE
N_HEADS = 16
HEAD_DIM = 128
N_KV_HEADS = 4
GROUP = 4
CMP_BLOCK = 32
CMP_STRIDE = 16
SEL_BLOCK = 64
N_SEL = 16
WINDOW = 512
Q_BLOCK = 128
SCALE = HEAD_DIM ** -0.5
KV_COLS = 2 * N_KV_HEADS * HEAD_DIM
ROW_SPLIT = 2 * N_KV_HEADS
N_Q_COLS = N_HEADS * HEAD_DIM
N_GATE_COLS = N_HEADS * 3
POOL_WINDOWS = (2, 4, 8, 16)
POOL_GROUP = D_MODEL // 4
POOL_BUF = 15
POOL_EXT = 24
D_FF = 11 * D_MODEL // 4
N_EXPERTS = 8
TOP_K = 2
D_FF_EXPERT = 7 * D_MODEL // 2
MOE_BLOCK = 256
ALPHA = (2.0 * DEPTH) ** 0.25
LN_EPS = 1e-5
BIG = 1e9
NEG = -1e30

T_P = BATCH * SEQ
T_S = DEC_BATCH * DEC_SEQ
T_ALL = T_P + T_S
N_CMP = 128
SEL_TILE = 512
WIN_TILE = 256
FLASH_GROUPS = 2
VMEM_LIMIT = 56 * 1024 * 1024

_NT = (((1,), (1,)), ((), ()))


def _params(sem, vmem=VMEM_LIMIT):
    return pltpu.CompilerParams(dimension_semantics=sem, vmem_limit_bytes=vmem)


def _layer_norm(z, g, b):
    mu = jnp.mean(z, axis=-1, keepdims=True)
    zc = z - mu
    var = jnp.mean(zc * zc, axis=-1, keepdims=True)
    return zc * lax.rsqrt(var + LN_EPS) * g + b


def _pick(i, n_prompt_tiles, p_ref, s_ref):
    return jnp.where(i < n_prompt_tiles, p_ref[...], s_ref[...])


def _mod_specs(k, tm):
    tpb = SEQ // tm
    npt = T_P // tm
    return (pl.BlockSpec((None, 1, D_MODEL), lambda i, *_: (jnp.minimum(i // tpb, BATCH - 1), 0, k)),
            pl.BlockSpec((tm, D_MODEL), lambda i, *_: (jnp.maximum(i - npt, 0), k)))


def _split_bf16(x):
    hi = x.astype(BF16)
    lo = (x - hi.astype(F32)).astype(BF16)
    return hi, lo


def _ada_kernel(c_ref, w_ref, b_ref, o_ref):
    c = c_ref[...]
    a = (c * jax.nn.sigmoid(c)).astype(BF16)
    o_ref[0] = jnp.dot(a, w_ref[0].astype(BF16), preferred_element_type=F32) + b_ref[0]


def _ada(c_all, w_ada, b_ada):
    nb = c_all.shape[0]
    tn = 1024
    return pl.pallas_call(
        _ada_kernel,
        grid=(DEPTH, 6 * D_MODEL // tn),
        in_specs=[pl.BlockSpec((nb, D_MODEL), lambda l, j: (0, 0)),
                  pl.BlockSpec((1, D_MODEL, tn), lambda l, j: (l, 0, j)),
                  pl.BlockSpec((1, 1, tn), lambda l, j: (l, 0, j))],
        out_specs=pl.BlockSpec((1, nb, tn), lambda l, j: (l, 0, j)),
        out_shape=jax.ShapeDtypeStruct((DEPTH, nb, 6 * D_MODEL), F32),
        compiler_params=_params(("arbitrary", "arbitrary")),
        name="adaln",
    )(c_all, w_ada, b_ada.reshape(DEPTH, 1, 6 * D_MODEL))


def _x_specs(tm):
    npt = T_P // tm
    return (pl.BlockSpec((tm, D_MODEL), lambda i: (jnp.minimum(i, npt - 1), 0)),
            pl.BlockSpec((tm, D_MODEL), lambda i: (jnp.maximum(i - npt, 0), 0)))


def _modulate_kernel(npt, xp_ref, xs_ref, shp, shs, scp, scs, h_ref):
    i = pl.program_id(0)
    shift = _pick(i, npt, shp, shs)
    scale = _pick(i, npt, scp, scs)
    h_ref[...] = (_pick(i, npt, xp_ref, xs_ref) * (1.0 + scale) + shift).astype(BF16)


def _modulate(xp, xs, mp, ms, k_shift, k_scale, tm=256):
    return pl.pallas_call(
        functools.partial(_modulate_kernel, T_P // tm),
        grid=(T_ALL // tm,),
        in_specs=[*_x_specs(tm), *_mod_specs(k_shift, tm), *_mod_specs(k_scale, tm)],
        out_specs=pl.BlockSpec((tm, D_MODEL), lambda i: (i, 0)),
        out_shape=jax.ShapeDtypeStruct((T_ALL, D_MODEL), BF16),
        compiler_params=_params(("parallel",)),
        name="modulate",
    )(xp, xs, mp, ms, mp, ms)


def _proj_q_kernel(x_ref, w_ref, q_ref):
    q_ref[...] = (jnp.dot(x_ref[...], w_ref[...], preferred_element_type=F32) * SCALE).astype(BF16)


def _store_kv_rows(ref, r):
    for c in range(ROW_SPLIT):
        ref[pl.ds(c, r.shape[0], stride=ROW_SPLIT), :] = r[:, c * HEAD_DIM:(c + 1) * HEAD_DIM]


def _proj_kv_kernel(npt, x_ref, w_ref, kvp_ref, kvs_ref, kvb_ref):
    i = pl.program_id(0)
    r = jnp.dot(x_ref[...], w_ref[...], preferred_element_type=F32)
    kvb_ref[...] = r.astype(BF16)

    @pl.when(i < npt)
    def _():
        _store_kv_rows(kvp_ref, r)

    @pl.when(i >= npt)
    def _():
        _store_kv_rows(kvs_ref, r)


def _proj_gate_kernel(x_ref, w_ref, g_ref):
    g_ref[...] = jax.nn.sigmoid(jnp.dot(x_ref[...], w_ref[...], preferred_element_type=F32))


def _nsa_project(h, w_in_bf, w_gate_bf, tm=1024, tn=1024):
    nm = T_ALL // tm
    q = pl.pallas_call(
        _proj_q_kernel,
        grid=(nm, N_Q_COLS // tn),
        in_specs=[pl.BlockSpec((tm, D_MODEL), lambda i, j: (i, 0)),
                  pl.BlockSpec((D_MODEL, tn), lambda i, j: (0, j))],
        out_specs=pl.BlockSpec((tm, tn), lambda i, j: (i, j)),
        out_shape=jax.ShapeDtypeStruct((T_ALL, N_Q_COLS), BF16),
        compiler_params=_params(("parallel", "arbitrary")),
        name="nsa_proj_q",
    )(h, w_in_bf)
    npt = T_P // tm
    rows = tm * ROW_SPLIT
    branches = []
    for br in range(3):
        wcol = N_Q_COLS // KV_COLS + br
        branches.append(pl.pallas_call(
            functools.partial(_proj_kv_kernel, npt),
            grid=(nm,),
            in_specs=[pl.BlockSpec((tm, D_MODEL), lambda i: (i, 0)),
                      pl.BlockSpec((D_MODEL, KV_COLS), functools.partial(lambda c, i: (0, c), wcol))],
            out_specs=[pl.BlockSpec((rows, HEAD_DIM), lambda i: (jnp.minimum(i, npt - 1), 0)),
                       pl.BlockSpec((rows, HEAD_DIM), lambda i: (jnp.maximum(i - npt, 0), 0)),
                       pl.BlockSpec((tm, KV_COLS), lambda i: (i, 0))],
            out_shape=[jax.ShapeDtypeStruct((T_P * ROW_SPLIT, HEAD_DIM), F32),
                       jax.ShapeDtypeStruct((T_S * ROW_SPLIT, HEAD_DIM), F32),
                       jax.ShapeDtypeStruct((T_ALL, KV_COLS), BF16)],
            compiler_params=_params(("arbitrary",)),
            name=f"nsa_proj_kv{br}",
        )(h, w_in_bf))
    gates = pl.pallas_call(
        _proj_gate_kernel,
        grid=(nm,),
        in_specs=[pl.BlockSpec((tm, D_MODEL), lambda i: (i, 0)),
                  pl.BlockSpec((D_MODEL, 128), lambda i: (0, 0))],
        out_specs=pl.BlockSpec((tm, 128), lambda i: (i, 0)),
        out_shape=jax.ShapeDtypeStruct((T_ALL, 128), F32),
        compiler_params=_params(("parallel",)),
        name="nsa_proj_gate",
    )(h, w_gate_bf)
    return q, branches, gates


CHUNK_PITCH = CMP_STRIDE * ROW_SPLIT


def _gelu_tanh(x):
    return 0.5 * x * (1.0 + jnp.tanh(0.7978845608028654 * (x + 0.044715 * x * x * x)))


def _compress_core(load, wc_ref, w1_ref, w2_ref, cpos_ref, out_ref):
    n_chunk = PAST_LEN // CMP_STRIDE
    for k2 in range(2):
        cols = []
        for i in range(CMP_STRIDE):
            heads = [load(i, k2 * N_KV_HEADS + h) for h in range(N_KV_HEADS)]
            cols.append(jnp.concatenate(heads, axis=0).astype(BF16))
        x = jnp.concatenate(cols, axis=1)
        part = jnp.dot(x, wc_ref[k2], preferred_element_type=F32)
        nrow = N_KV_HEADS * n_chunk
        hsum = part[:, :HEAD_DIM] + pltpu.roll(part[:, HEAD_DIM:], nrow - 1, axis=0)
        pos_bias = jnp.dot(cpos_ref[k2].astype(BF16), w1_ref[k2], preferred_element_type=F32)[0:1]
        hid = _gelu_tanh(hsum + pos_bias)
        kvc = jnp.dot(hid.astype(BF16), w2_ref[k2], preferred_element_type=F32)
        for h in range(N_KV_HEADS):
            c0 = (k2 * N_KV_HEADS + h) * HEAD_DIM
            out_ref[0, :, c0:c0 + HEAD_DIM] = kvc[h * n_chunk:(h + 1) * n_chunk].astype(BF16)


def _compress_prompt_kernel(x_ref, wc_ref, w1_ref, w2_ref, cpos_ref, out_ref):
    n_chunk = SEQ // CMP_STRIDE

    def load(i, c):
        return x_ref[pl.ds(ROW_SPLIT * i + c, n_chunk, stride=CHUNK_PITCH), :]

    _compress_core(load, wc_ref, w1_ref, w2_ref, cpos_ref, out_ref)


def _compress_sample_kernel(pt_ref, *refs):
    pages = refs[:N_PAGES]
    wc_ref, w1_ref, w2_ref, cpos_ref, out_ref, xt = refs[N_PAGES:]
    cpp = PAGE_SIZE // CMP_STRIDE
    for p, pg in enumerate(pages):
        xt[:, :, p * cpp:(p + 1) * cpp, :] = pltpu.einshape(
            "nicd->icnd", pg[...].reshape(cpp, CMP_STRIDE, ROW_SPLIT, HEAD_DIM))

    _compress_core(lambda i, c: xt[i, c], wc_ref, w1_ref, w2_ref, cpos_ref, out_ref)


def _compress_weight_specs():
    z3 = lambda *a: (0, 0, 0)
    return [pl.BlockSpec((2, CMP_STRIDE * HEAD_DIM, 2 * HEAD_DIM), z3),
            pl.BlockSpec((2, CMP_BLOCK * HEAD_DIM, HEAD_DIM), z3),
            pl.BlockSpec((2, HEAD_DIM, HEAD_DIM), z3),
            pl.BlockSpec((2, 8, CMP_BLOCK * HEAD_DIM), z3)]


def _compress_prompt(kv_rows, cw):
    return pl.pallas_call(
        _compress_prompt_kernel,
        grid=(BATCH,),
        in_specs=[pl.BlockSpec((SEQ * ROW_SPLIT, HEAD_DIM), lambda b: (b, 0)), *_compress_weight_specs()],
        out_specs=pl.BlockSpec((1, N_CMP, KV_COLS), lambda b: (b, 0, 0)),
        out_shape=jax.ShapeDtypeStruct((BATCH, N_CMP, KV_COLS), BF16),
        compiler_params=_params(("parallel",)),
        name="compress_prompt",
    )(kv_rows, *cw)


def _page_specs():
    return [pl.BlockSpec((PAGE_SIZE * ROW_SPLIT, HEAD_DIM), functools.partial(lambda p, s, pt: (pt[s, p], 0), p))
            for p in range(N_PAGES)]


def _compress_sample(cache_rows, page_table, cw):
    grid_spec = pltpu.PrefetchScalarGridSpec(
        num_scalar_prefetch=1,
        grid=(DEC_BATCH,),
        in_specs=[*_page_specs(), *_compress_weight_specs()],
        out_specs=pl.BlockSpec((1, N_CMP, KV_COLS), lambda s, pt: (s, 0, 0)),
        scratch_shapes=[pltpu.VMEM((CMP_STRIDE, ROW_SPLIT, PAST_LEN // CMP_STRIDE, HEAD_DIM), F32)],
    )
    return pl.pallas_call(
        _compress_sample_kernel,
        grid_spec=grid_spec,
        out_shape=jax.ShapeDtypeStruct((DEC_BATCH, N_CMP, KV_COLS), BF16),
        compiler_params=_params(("arbitrary",)),
        name="compress_sample",
    )(page_table, *([cache_rows] * N_PAGES), *cw)


def _masked_exp(s, mask):
    sm = jnp.where(mask, s, NEG)
    m = jnp.max(sm, axis=-1, keepdims=True)
    e = jnp.where(mask, jnp.exp(sm - m), 0.0)
    return e, jnp.sum(e, axis=-1, keepdims=True)


def _rep4(a):
    return jnp.concatenate([a] * GROUP, axis=0)


def _cmp_and_select(qg, kc, vc, ovt, qpos_col, qpos_row, n_tok):
    nsp = ovt.shape[0]
    s = lax.dot_general(qg, kc, _NT, preferred_element_type=F32)
    n_idx = lax.broadcasted_iota(jnp.int32, (n_tok, N_CMP), 1)
    vis = (n_idx * CMP_STRIDE + (CMP_BLOCK - 1) <= qpos_col) & (n_idx < N_CMP - 1)
    e, l = _masked_exp(s, _rep4(vis))
    p = e * (1.0 / jnp.maximum(l, 1e-30))
    o_c = jnp.dot(p.astype(BF16), vc, preferred_element_type=F32)
    psum = p[0:n_tok] + p[n_tok:2 * n_tok] + p[2 * n_tok:3 * n_tok] + p[3 * n_tok:4 * n_tok]
    if n_tok < 128:
        psum = jnp.concatenate([psum, jnp.zeros((128 - n_tok, N_CMP), F32)], axis=0)
    hi, lo = _split_bf16(psum)
    score = (lax.dot_general(ovt, hi, _NT, preferred_element_type=F32)
             + lax.dot_general(ovt, lo, _NT, preferred_element_type=F32))
    m_idx = lax.broadcasted_iota(jnp.int32, (nsp, 128), 0)
    cur = jnp.right_shift(qpos_row, 6)
    forced = (m_idx == 0) | (m_idx == cur) | (m_idx == cur - 1)
    score = jnp.where(forced, BIG, score)
    score = jnp.where(m_idx > cur, -BIG, score)
    rank = jnp.zeros((nsp, 128), F32)
    for mp in range(nsp):
        row = score[mp:mp + 1, :]
        beats = (row > score) | ((row == score) & (m_idx > mp))
        rank = rank + beats.astype(F32)
    sel_t = ((rank < N_SEL) & (score > -0.5 * BIG)).astype(F32)
    sel_t = jnp.concatenate([sel_t, jnp.zeros((128 - nsp, 128), F32)], axis=0)
    return o_c, sel_t.T.astype(BF16)


def _block_expand(n_keys, key0):
    m_idx = lax.broadcasted_iota(jnp.int32, (128, n_keys), 0)
    c_idx = lax.broadcasted_iota(jnp.int32, (128, n_keys), 1)
    return (m_idx == jnp.right_shift(key0 + c_idx, 6)).astype(BF16)


def _flash(qgs, k_ref, v_ref, cols, lo, hi, mask_fns, tile):
    rows = qgs[0].shape[0]
    ones = jnp.ones((tile, HEAD_DIM), BF16)
    voff = N_KV_HEADS * HEAD_DIM

    def step(qg, col, mask_fn, kt, carry):
        m, l, acc = carry
        r0 = pl.multiple_of(kt * tile, tile)
        k = k_ref[pl.ds(r0, tile), col:col + HEAD_DIM]
        v = v_ref[pl.ds(r0, tile), voff + col:voff + col + HEAD_DIM]
        s = lax.dot_general(qg, k, _NT, preferred_element_type=F32)
        bias = jnp.where(mask_fn(kt * tile), 0.0, NEG)
        sm = s + _rep4(bias)
        m_new = jnp.maximum(m, jnp.max(sm, axis=-1, keepdims=True))
        alpha = jnp.exp(m - m_new)
        e = jnp.exp(sm - jnp.concatenate([m_new] * (tile // HEAD_DIM), axis=1)).astype(BF16)
        pv = jnp.dot(e, jnp.concatenate([v, ones], axis=1), preferred_element_type=F32)
        return m_new, alpha * l + pv[:, HEAD_DIM:], alpha * acc + pv[:, :HEAD_DIM]

    def body(kt, carries):
        return tuple(step(qg, col, fn, kt, c) for qg, col, fn, c in zip(qgs, cols, mask_fns, carries))

    zero = jnp.zeros((rows, HEAD_DIM), F32)
    outs = lax.fori_loop(lo, hi, body, ((jnp.full((rows, HEAD_DIM), NEG, F32), zero, zero),) * len(qgs))
    return [acc * (1.0 / l) for _, l, acc in outs]


def _combine_heads(g, n_tok, o_c, o_s, o_w, gates, o_ref):
    for j in range(GROUP):
        hd = g * GROUP + j
        rows = slice(j * n_tok, (j + 1) * n_tok)
        o = (gates[:, 3 * hd:3 * hd + 1] * o_c[rows] + gates[:, 3 * hd + 1:3 * hd + 2] * o_s[rows]
             + gates[:, 3 * hd + 2:3 * hd + 3] * o_w[rows])
        o_ref[:, hd * HEAD_DIM:(hd + 1) * HEAD_DIM] = o.astype(o_ref.dtype)


def _attn_prompt_kernel(q_ref, ks_ref, kw_ref, kvc_ref, gate_ref, ovt_ref, o_ref):
    b = pl.program_id(0)

    @pl.when(b == BATCH)
    def _():
        o_ref[...] = jnp.zeros_like(o_ref)

    @pl.when(b < BATCH)
    def _():
        _attn_prompt_block(q_ref, ks_ref, kw_ref, kvc_ref, gate_ref, ovt_ref, o_ref)


def _attn_prompt_block(q_ref, ks_ref, kw_ref, kvc_ref, gate_ref, ovt_ref, o_ref):
    qb = pl.program_id(1)
    q0 = qb * Q_BLOCK
    qpos_col = q0 + lax.broadcasted_iota(jnp.int32, (Q_BLOCK, 1), 0)
    qpos_row = q0 + lax.broadcasted_iota(jnp.int32, (1, Q_BLOCK), 1)
    gates = gate_ref[...]
    ovt = ovt_ref[...]
    q_end = q0 + Q_BLOCK - 1

    def key_pos(key0, tile):
        return key0 + lax.broadcasted_iota(jnp.int32, (Q_BLOCK, tile), 1)

    def win_mask(key0):
        dist = qpos_col - key_pos(key0, WIN_TILE)
        return (dist >= 0) & (dist <= WINDOW)

    for g0 in range(0, N_KV_HEADS, FLASH_GROUPS):
        gs = range(g0, g0 + FLASH_GROUPS)
        cols = [g * HEAD_DIM for g in gs]
        qgs, o_cs, sel_masks = [], [], []
        for g, col in zip(gs, cols):
            qg = jnp.concatenate([q_ref[:, (g * GROUP + j) * HEAD_DIM:(g * GROUP + j + 1) * HEAD_DIM]
                                  for j in range(GROUP)], axis=0)
            vcol = N_KV_HEADS * HEAD_DIM + col
            o_c, sel = _cmp_and_select(qg, kvc_ref[0, :, col:col + HEAD_DIM], kvc_ref[0, :, vcol:vcol + HEAD_DIM],
                                       ovt, qpos_col, qpos_row, Q_BLOCK)

            def sel_mask(key0, sel=sel):
                chosen = jnp.dot(sel, _block_expand(SEL_TILE, key0), preferred_element_type=F32) > 0.5
                return chosen & (key_pos(key0, SEL_TILE) <= qpos_col)

            qgs.append(qg)
            o_cs.append(o_c)
            sel_masks.append(sel_mask)
        o_ss = _flash(qgs, ks_ref, ks_ref, cols, 0, q_end // SEL_TILE + 1, sel_masks, SEL_TILE)
        o_ws = _flash(qgs, kw_ref, kw_ref, cols, jnp.maximum(q0 - WINDOW, 0) // WIN_TILE, q_end // WIN_TILE + 1,
                      [win_mask] * FLASH_GROUPS, WIN_TILE)
        for g, o_c, o_s, o_w in zip(gs, o_cs, o_ss, o_ws):
            _combine_heads(g, Q_BLOCK, o_c, o_s, o_w, gates, o_ref)


def _attn_prompt(q, kvb_sel, kvb_win, kvc, gates, ovt):
    nqb = SEQ // Q_BLOCK
    n_fill = T_S // Q_BLOCK
    last = BATCH - 1
    qrow = lambda b, i: (jnp.minimum(b, last) * nqb + i, 0)
    orow = lambda b, i: (jnp.where(b < BATCH, b * nqb + i, BATCH * nqb + jnp.minimum(i, n_fill - 1)), 0)
    return pl.pallas_call(
        _attn_prompt_kernel,
        grid=(BATCH + 1, nqb),
        in_specs=[pl.BlockSpec((Q_BLOCK, N_Q_COLS), qrow),
                  pl.BlockSpec((SEQ, KV_COLS), lambda b, i: (jnp.minimum(b, last), 0)),
                  pl.BlockSpec((SEQ, KV_COLS), lambda b, i: (jnp.minimum(b, last), 0)),
                  pl.BlockSpec((1, N_CMP, KV_COLS), lambda b, i: (jnp.minimum(b, last), 0, 0)),
                  pl.BlockSpec((Q_BLOCK, 128), qrow),
                  pl.BlockSpec(ovt.shape, lambda b, i: (0, 0))],
        out_specs=pl.BlockSpec((Q_BLOCK, N_Q_COLS), orow),
        out_shape=jax.ShapeDtypeStruct((T_ALL, N_Q_COLS), BF16),
        compiler_params=_params(("arbitrary", "arbitrary")),
        name="attn_prompt",
    )(q, kvb_sel, kvb_win, kvc, gates, ovt)


TAIL = 128
SEL_KEYS = PAST_LEN + TAIL
WIN_KEYS = WINDOW + TAIL
WIN_ROWS = WINDOW * ROW_SPLIT
NEW_ROWS = DEC_SEQ * ROW_SPLIT


def _attn_sample_kernel(pt_ref, *refs):
    pages = refs[:N_PAGES]
    q_ref, news_ref, neww_ref, win_ref, kvc_ref, gate_ref, ovt_ref, o_ref, nwin_ref, sbuf, wbuf = refs[N_PAGES:]
    def by_group(ref):
        x = ref[...]
        return pltpu.einshape("tcd->ctd", x.reshape(x.shape[0] // ROW_SPLIT, ROW_SPLIT, HEAD_DIM))

    pad = jnp.zeros((ROW_SPLIT, TAIL - DEC_SEQ, HEAD_DIM), F32)
    for p in range(N_PAGES):
        sbuf[:, p * PAGE_SIZE:(p + 1) * PAGE_SIZE, :] = by_group(pages[p]).astype(BF16)
    sbuf[:, PAST_LEN:, :] = jnp.concatenate([by_group(news_ref), pad], axis=1).astype(BF16)
    wbuf[:, :WINDOW, :] = by_group(win_ref).astype(BF16)
    wbuf[:, WINDOW:, :] = jnp.concatenate([by_group(neww_ref), pad], axis=1).astype(BF16)
    nwin_ref[:WIN_ROWS - NEW_ROWS, :] = win_ref[NEW_ROWS:, :]
    nwin_ref[WIN_ROWS - NEW_ROWS:, :] = neww_ref[...]

    tcol = lax.broadcasted_iota(jnp.int32, (DEC_SEQ, 1), 0)
    qpos_col = PAST_LEN + tcol
    qpos_row = PAST_LEN + jnp.minimum(lax.broadcasted_iota(jnp.int32, (1, 128), 1), DEC_SEQ - 1)
    gates = gate_ref[...]
    ovt = ovt_ref[...]
    expand = _block_expand(SEL_KEYS, 0)
    sel_pos = lax.broadcasted_iota(jnp.int32, (DEC_SEQ, SEL_KEYS), 1)
    win_c = lax.broadcasted_iota(jnp.int32, (DEC_SEQ, WIN_KEYS), 1)
    win_mask = _rep4((win_c >= tcol) & (win_c <= WINDOW + tcol))
    for g in range(N_KV_HEADS):
        col = g * HEAD_DIM
        vcol = N_KV_HEADS * HEAD_DIM + col
        qg = jnp.concatenate([q_ref[:, (g * GROUP + j) * HEAD_DIM:(g * GROUP + j + 1) * HEAD_DIM]
                              for j in range(GROUP)], axis=0).astype(BF16)
        o_c, sel = _cmp_and_select(qg, kvc_ref[0, :, col:col + HEAD_DIM], kvc_ref[0, :, vcol:vcol + HEAD_DIM],
                                   ovt, qpos_col, qpos_row, DEC_SEQ)
        chosen = jnp.dot(sel, expand, preferred_element_type=F32)[:DEC_SEQ] > 0.5
        mask = _rep4(chosen & (sel_pos <= qpos_col))
        s = lax.dot_general(qg, sbuf[g], _NT, preferred_element_type=F32)
        e, l = _masked_exp(s, mask)
        o_s = jnp.dot(e.astype(BF16), sbuf[N_KV_HEADS + g], preferred_element_type=F32)
        o_s = o_s / jnp.maximum(l, 1e-30)
        s = lax.dot_general(qg, wbuf[g], _NT, preferred_element_type=F32)
        e, l = _masked_exp(s, win_mask)
        o_w = jnp.dot(e.astype(BF16), wbuf[N_KV_HEADS + g], preferred_element_type=F32)
        o_w = o_w / jnp.maximum(l, 1e-30)
        _combine_heads(g, DEC_SEQ, o_c, o_s, o_w, gates, o_ref)


def _attn_sample(q_s, new_sel, new_win, cache_sel, state_win, page_table, kvc, gates, ovt):
    row0 = T_P // DEC_SEQ
    grid_spec = pltpu.PrefetchScalarGridSpec(
        num_scalar_prefetch=1,
        grid=(DEC_BATCH,),
        in_specs=[*_page_specs(),
                  pl.BlockSpec((DEC_SEQ, N_Q_COLS), lambda s, pt: (s, 0)),
                  pl.BlockSpec((NEW_ROWS, HEAD_DIM), lambda s, pt: (s, 0)),
                  pl.BlockSpec((NEW_ROWS, HEAD_DIM), lambda s, pt: (s, 0)),
                  pl.BlockSpec((WIN_ROWS, HEAD_DIM), lambda s, pt: (s, 0)),
                  pl.BlockSpec((1, N_CMP, KV_COLS), lambda s, pt: (s, 0, 0)),
                  pl.BlockSpec((DEC_SEQ, 128), lambda s, pt: (row0 + s, 0)),
                  pl.BlockSpec(ovt.shape, lambda s, pt: (0, 0))],
        out_specs=[pl.BlockSpec((DEC_SEQ, N_Q_COLS), lambda s, pt: (s, 0)),
                   pl.BlockSpec((WIN_ROWS, HEAD_DIM), lambda s, pt: (s, 0))],
        scratch_shapes=[pltpu.VMEM((ROW_SPLIT, SEL_KEYS, HEAD_DIM), BF16),
                        pltpu.VMEM((ROW_SPLIT, WIN_KEYS, HEAD_DIM), BF16)],
    )
    return pl.pallas_call(
        _attn_sample_kernel,
        grid_spec=grid_spec,
        out_shape=[jax.ShapeDtypeStruct((T_S, N_Q_COLS), F32),
                   jax.ShapeDtypeStruct((DEC_BATCH * WIN_ROWS, HEAD_DIM), F32)],
        compiler_params=_params(("arbitrary",)),
        name="attn_sample",
    )(page_table, *([cache_sel] * N_PAGES), q_s, new_sel, new_win, state_win, kvc, gates, ovt)


def _overlap_t(ns, nsp):
    c0 = jnp.arange(N_CMP)[None, :] * CMP_STRIDE
    s0 = jnp.arange(nsp)[:, None] * SEL_BLOCK
    ov = jnp.minimum(c0 + CMP_BLOCK, s0 + SEL_BLOCK) - jnp.maximum(c0, s0)
    ov = jnp.maximum(ov, 0).astype(F32) / CMP_BLOCK
    ov = jnp.where((jnp.arange(nsp)[:, None] < ns) & (jnp.arange(N_CMP)[None, :] < N_CMP - 1), ov, 0.0)
    return ov.astype(BF16)


def _outproj_kernel(npt, o_ref, w_ref, xp_ref, xs_ref, gp, gs, shp, shs, scp, scs, lng, lnb, x1_ref, h1_ref):
    i = pl.program_id(0)
    y = jnp.dot(o_ref[...], w_ref[...], preferred_element_type=F32)
    x = _pick(i, npt, xp_ref, xs_ref)
    x1 = _layer_norm(ALPHA * x + _pick(i, npt, gp, gs) * y, lng[...], lnb[...])
    x1_ref[...] = x1
    h1_ref[...] = (x1 * (1.0 + _pick(i, npt, scp, scs)) + _pick(i, npt, shp, shs)).astype(BF16)


def _outproj(o, w_out_bf, xp, xs, mp, ms, lng, lnb, tm=256):
    row = pl.BlockSpec((tm, D_MODEL), lambda i: (i, 0))
    vec = pl.BlockSpec((1, D_MODEL), lambda i: (0, 0))
    return pl.pallas_call(
        functools.partial(_outproj_kernel, T_P // tm),
        grid=(T_ALL // tm,),
        in_specs=[row, pl.BlockSpec((N_Q_COLS, D_MODEL), lambda i: (0, 0)), *_x_specs(tm),
                  *_mod_specs(2, tm), *_mod_specs(3, tm), *_mod_specs(4, tm), vec, vec],
        out_specs=[row, row],
        out_shape=[jax.ShapeDtypeStruct((T_ALL, D_MODEL), F32), jax.ShapeDtypeStruct((T_ALL, D_MODEL), BF16)],
        compiler_params=_params(("parallel",)),
        name="nsa_out_postnorm",
    )(o, w_out_bf, xp, xs, mp, ms, mp, ms, mp, ms, lng, lnb)


def _swiglu_up_kernel(x_ref, wg_ref, wu_ref, a_ref):
    x = x_ref[...]
    g = jnp.dot(x, wg_ref[...], preferred_element_type=F32)
    u = jnp.dot(x, wu_ref[...], preferred_element_type=F32)
    a_ref[...] = (g * jax.nn.sigmoid(g) * u).astype(BF16)


def _ffn_up(h, w_up_bf, tm=1024, tn=512):
    nj = D_FF // tn
    return pl.pallas_call(
        _swiglu_up_kernel,
        grid=(T_ALL // tm, nj),
        in_specs=[pl.BlockSpec((tm, D_MODEL), lambda i, j: (i, 0)),
                  pl.BlockSpec((D_MODEL, tn), lambda i, j: (0, j)),
                  pl.BlockSpec((D_MODEL, tn), lambda i, j: (0, nj + j))],
        out_specs=pl.BlockSpec((tm, tn), lambda i, j: (i, j)),
        out_shape=jax.ShapeDtypeStruct((T_ALL, D_FF), BF16),
        compiler_params=_params(("parallel", "arbitrary")),
        name="ffn_up",
    )(h, w_up_bf, w_up_bf)


def _ffn_down_kernel(npt, a_ref, w_ref, x_ref, gp, gs, lng, lnb, x2_ref, acc):
    i = pl.program_id(0)
    k = pl.program_id(1)

    @pl.when(k == 0)
    def _():
        acc[...] = jnp.zeros_like(acc)

    acc[...] += jnp.dot(a_ref[...], w_ref[...], preferred_element_type=F32)

    @pl.when(k == pl.num_programs(1) - 1)
    def _():
        x2_ref[...] = _layer_norm(ALPHA * x_ref[...] + _pick(i, npt, gp, gs) * acc[...], lng[...], lnb[...])


def _ffn_down(a, w_down_bf, x, mp, ms, lng, lnb, tm=512, tk=1408):
    row = pl.BlockSpec((tm, D_MODEL), lambda i, k: (i, 0))
    vec = pl.BlockSpec((1, D_MODEL), lambda i, k: (0, 0))
    return pl.pallas_call(
        functools.partial(_ffn_down_kernel, T_P // tm),
        grid=(T_ALL // tm, D_FF // tk),
        in_specs=[pl.BlockSpec((tm, tk), lambda i, k: (i, k)),
                  pl.BlockSpec((tk, D_MODEL), lambda i, k: (k, 0)),
                  row, *_mod_specs(5, tm), vec, vec],
        out_specs=row,
        out_shape=jax.ShapeDtypeStruct((T_ALL, D_MODEL), F32),
        scratch_shapes=[pltpu.VMEM((tm, D_MODEL), F32)],
        compiler_params=_params(("parallel", "arbitrary")),
        name="ffn_down_postnorm",
    )(a, w_down_bf, x, mp, ms, lng, lnb)


HALO = 16


def _pool_kernel(prompt, tiles_per_seq, n_tiles, *refs):
    i = pl.program_id(0)
    x1_ref, h1_ref, _, lg_ref = refs[-4:]

    @pl.when(i >= n_tiles)
    def _():
        x1_ref[...] = jnp.zeros_like(x1_ref)
        h1_ref[...] = jnp.zeros_like(h1_ref)
        lg_ref[...] = jnp.zeros_like(lg_ref)

    @pl.when(i < n_tiles)
    def _():
        _pool_tile(prompt, tiles_per_seq, *refs)


def _pool_tile(prompt, tiles_per_seq, x_ref, halo_ref, sh_m, sc_m, g_m, sh_f, sc_f, hsh, hsc,
               wp_ref, ps_ref, lng, lnb, wr_hi, wr_lo, br_ref, x1_ref, h1_ref, hm_ref, lg_ref):
    i = pl.program_id(0)
    tm = x_ref.shape[0]
    x = x_ref[...]
    h = x * (1.0 + sc_m[...]) + sh_m[...]
    hm_ref[...] = h
    if prompt:
        hh = halo_ref[...] * (1.0 + hsc[...]) + hsh[...]
        hh = jnp.where(i % tiles_per_seq == 0, 0.0, hh)
        pos = (i % tiles_per_seq) * tm + lax.broadcasted_iota(jnp.int32, (tm, 1), 0)
    else:
        hh = jnp.zeros((HALO, D_MODEL), F32)
    ext = jnp.concatenate([hh, h], axis=0)
    ys = []
    for gi, w in enumerate(POOL_WINDOWS):
        a = ext[:, gi * POOL_GROUP:(gi + 1) * POOL_GROUP]
        s = a
        d = 1
        while d < w:
            s = s + pltpu.roll(s, d, axis=0)
            d *= 2
        s = s[HALO:]
        if prompt:
            inv = 1.0 / jnp.minimum(pos + 1, w).astype(F32)
        else:
            inv = 1.0 / w
        diff = (s * inv - a[HALO:]).astype(BF16)
        ys.append(jnp.dot(diff, wp_ref[gi], preferred_element_type=F32))
    y = jnp.concatenate(ys, axis=1) * ps_ref[...]
    x1 = _layer_norm(ALPHA * x + g_m[...] * y, lng[...], lnb[...])
    x1_ref[...] = x1
    h1 = x1 * (1.0 + sc_f[...]) + sh_f[...]
    h1_ref[...] = h1
    hi, lo = _split_bf16(h1)
    lg_ref[...] = (jnp.dot(hi, wr_hi[...], preferred_element_type=F32)
                   + jnp.dot(hi, wr_lo[...], preferred_element_type=F32)
                   + jnp.dot(lo, wr_hi[...], preferred_element_type=F32) + br_ref[...])


def _pool_call(prompt, x, n, n_out, hm_spec, hm_rows, mods, w_pool_bf, pool_scale, lng, lnb, wr_hi, wr_lo,
               b_router, tm):
    row = pl.BlockSpec((tm, D_MODEL), lambda i: (i, 0))
    vec = pl.BlockSpec((1, D_MODEL), lambda i: (0, 0))
    halo = pl.BlockSpec((HALO, D_MODEL), lambda i: (jnp.maximum(i * (tm // HALO) - 1, 0), 0))
    specs = [s for s, _ in mods]
    arrs = [a for _, a in mods]
    return pl.pallas_call(
        functools.partial(_pool_kernel, prompt, SEQ // tm, n // tm),
        grid=(n_out // tm,),
        in_specs=[row, halo, *specs,
                  pl.BlockSpec((4, POOL_GROUP, POOL_GROUP), lambda i: (0, 0, 0)), vec, vec, vec,
                  pl.BlockSpec((D_MODEL, 128), lambda i: (0, 0)), pl.BlockSpec((D_MODEL, 128), lambda i: (0, 0)),
                  pl.BlockSpec((1, 128), lambda i: (0, 0))],
        out_specs=[row, row, hm_spec, pl.BlockSpec((tm, 128), lambda i: (i, 0))],
        out_shape=[jax.ShapeDtypeStruct((n_out, D_MODEL), F32), jax.ShapeDtypeStruct((n_out, D_MODEL), F32),
                   jax.ShapeDtypeStruct((hm_rows, D_MODEL), F32), jax.ShapeDtypeStruct((n_out, 128), F32)],
        compiler_params=_params(("arbitrary",)),
        name="pool_prompt" if prompt else "pool_sample",
    )(x, x, *arrs, w_pool_bf, pool_scale, lng, lnb, wr_hi, wr_lo, b_router)


_POOL_MOD_CHUNKS = (0, 1, 2, 3, 4, 0, 1)


def _pool_prompt(x, mp, *rest, tm=256):
    tpb = SEQ // tm
    seq = lambda i: jnp.minimum(i // tpb, BATCH - 1)
    mods = [(pl.BlockSpec((None, 1, D_MODEL), functools.partial(lambda k, i: (seq(i), 0, k), k)), mp)
            for k in _POOL_MOD_CHUNKS]
    hm_spec = pl.BlockSpec((tm, D_MODEL), lambda i: (seq(i), 0))
    return _pool_call(True, x, T_P, T_ALL, hm_spec, BATCH * tm, mods, *rest, tm)


def _pool_sample(x_ext, ms_ext, *rest, tm=8 * POOL_EXT):
    n = DEC_BATCH * POOL_EXT
    mods = [(pl.BlockSpec((tm, D_MODEL), functools.partial(lambda k, i: (i, k), k)), ms_ext)
            for k in _POOL_MOD_CHUNKS[:5]]
    mods += [(pl.BlockSpec((HALO, D_MODEL), functools.partial(lambda k, i: (0, k), k)), ms_ext)
             for k in _POOL_MOD_CHUNKS[5:]]
    hm_spec = pl.BlockSpec((tm, D_MODEL), lambda i: (i, 0))
    return _pool_call(False, x_ext, n, n, hm_spec, n, mods, *rest, tm)


N_MOE_BLOCKS = -(-(T_ALL * TOP_K) // MOE_BLOCK) + N_EXPERTS
N_SLOTS = N_MOE_BLOCKS * MOE_BLOCK


def _row_gather_kernel(idx_ref, src_hbm, o_ref, buf, sem):
    i = pl.program_id(0)
    nb = pl.num_programs(0)
    bsz = o_ref.shape[0]

    def issue(blk, slot):
        def body(r2, carry):
            for u in range(2):
                r = 2 * r2 + u
                row = idx_ref[blk * bsz + r]
                pltpu.make_async_copy(src_hbm.at[pl.ds(row, 1)], buf.at[slot, pl.ds(r, 1)],
                                      sem.at[slot]).start(priority=u)
            return carry
        lax.fori_loop(0, bsz // 2, body, 0, unroll=4)

    @pl.when(i == 0)
    def _():
        issue(0, 0)

    @pl.when(i + 1 < nb)
    def _():
        issue(i + 1, (i + 1) % 2)

    slot = i % 2
    pltpu.make_async_copy(src_hbm.at[pl.ds(0, bsz)], buf.at[slot], sem.at[slot]).wait()
    o_ref[...] = buf[slot].astype(BF16)


def _gather_rows_bf16(src, idx):
    grid_spec = pltpu.PrefetchScalarGridSpec(
        num_scalar_prefetch=1,
        grid=(N_MOE_BLOCKS,),
        in_specs=[pl.BlockSpec(memory_space=pl.ANY)],
        out_specs=pl.BlockSpec((MOE_BLOCK, D_MODEL), lambda i, idx: (i, 0)),
        scratch_shapes=[pltpu.VMEM((2, MOE_BLOCK, D_MODEL), F32), pltpu.SemaphoreType.DMA((2,))],
    )
    return pl.pallas_call(
        _row_gather_kernel,
        grid_spec=grid_spec,
        out_shape=jax.ShapeDtypeStruct((N_SLOTS, D_MODEL), BF16),
        compiler_params=_params(("arbitrary",)),
        name="moe_dispatch_gather",
    )(idx, src)


def _moe_up_kernel(be_ref, fresh_ref, live_ref, x_ref, wg_ref, wu_ref, a_ref, wg_s, wu_s):
    i = pl.program_id(1)

    @pl.when(fresh_ref[i] == 1)
    def _():
        wg_s[...] = wg_ref[...].astype(BF16)
        wu_s[...] = wu_ref[...].astype(BF16)

    @pl.when(live_ref[i] == 1)
    def _():
        x = x_ref[...]
        g = jnp.dot(x, wg_s[...], preferred_element_type=F32)
        u = jnp.dot(x, wu_s[...], preferred_element_type=F32)
        a_ref[...] = (g * jax.nn.sigmoid(g) * u).astype(BF16)

    @pl.when(live_ref[i] == 0)
    def _():
        a_ref[...] = jnp.zeros_like(a_ref)


def _moe_up(xb, w_up, block_expert, fresh, live, tn=1024):
    nj = D_FF_EXPERT // tn
    grid_spec = pltpu.PrefetchScalarGridSpec(
        num_scalar_prefetch=3,
        grid=(nj, N_MOE_BLOCKS),
        in_specs=[pl.BlockSpec((MOE_BLOCK, D_MODEL), lambda j, i, be, fr, lv: (i, 0)),
                  pl.BlockSpec((None, D_MODEL, tn), lambda j, i, be, fr, lv: (be[i], 0, j)),
                  pl.BlockSpec((None, D_MODEL, tn), lambda j, i, be, fr, lv: (be[i], 0, nj + j))],
        out_specs=pl.BlockSpec((MOE_BLOCK, tn), lambda j, i, be, fr, lv: (i, j)),
        scratch_shapes=[pltpu.VMEM((D_MODEL, tn), BF16), pltpu.VMEM((D_MODEL, tn), BF16)],
    )
    return pl.pallas_call(
        _moe_up_kernel,
        grid_spec=grid_spec,
        out_shape=jax.ShapeDtypeStruct((N_SLOTS, D_FF_EXPERT), BF16),
        compiler_params=_params(("arbitrary", "arbitrary")),
        name="moe_up",
    )(block_expert, fresh, live, xb, w_up, w_up)


def _moe_down_kernel(be_ref, fresh_ref, live_ref, a_ref, w_ref, sg_ref, y_ref, w_s):
    i = pl.program_id(1)

    @pl.when(fresh_ref[i] == 1)
    def _():
        w_s[...] = w_ref[...].astype(BF16)

    @pl.when(live_ref[i] == 1)
    def _():
        y_ref[...] = jnp.dot(a_ref[...], w_s[...], preferred_element_type=F32) * sg_ref[...]

    @pl.when(live_ref[i] == 0)
    def _():
        y_ref[...] = jnp.zeros_like(y_ref)


def _moe_down(act, w_down, slot_gate, block_expert, fresh, live, tn=512):
    grid_spec = pltpu.PrefetchScalarGridSpec(
        num_scalar_prefetch=3,
        grid=(D_MODEL // tn, N_MOE_BLOCKS),
        in_specs=[pl.BlockSpec((MOE_BLOCK, D_FF_EXPERT), lambda j, i, be, fr, lv: (i, 0)),
                  pl.BlockSpec((None, D_FF_EXPERT, tn), lambda j, i, be, fr, lv: (be[i], 0, j)),
                  pl.BlockSpec((MOE_BLOCK, 1), lambda j, i, be, fr, lv: (i, 0))],
        out_specs=pl.BlockSpec((MOE_BLOCK, tn), lambda j, i, be, fr, lv: (i, j)),
        scratch_shapes=[pltpu.VMEM((D_FF_EXPERT, tn), BF16)],
    )
    return pl.pallas_call(
        _moe_down_kernel,
        grid_spec=grid_spec,
        out_shape=jax.ShapeDtypeStruct((N_SLOTS, D_MODEL), F32),
        compiler_params=_params(("arbitrary", "arbitrary")),
        name="moe_down",
    )(block_expert, fresh, live, act, w_down, slot_gate)


def _route(logits):
    n = logits.shape[0]
    top_logit, top_e = lax.top_k(logits, TOP_K)
    gate = jax.nn.softmax(top_logit, axis=-1)
    e_flat = top_e.reshape(-1)
    order = jnp.argsort(e_flat, stable=True)
    e_sorted = e_flat[order]
    counts = jnp.bincount(e_flat, length=N_EXPERTS)
    padded = (counts + MOE_BLOCK - 1) // MOE_BLOCK * MOE_BLOCK
    start = jnp.cumsum(counts) - counts
    pend = jnp.cumsum(padded)
    pstart = pend - padded
    dest = (pstart[e_sorted] + jnp.arange(n * TOP_K) - start[e_sorted]).astype(jnp.int32)
    blk0 = jnp.arange(N_MOE_BLOCKS) * MOE_BLOCK
    block_expert = jnp.minimum(jnp.searchsorted(pend, blk0, side='right'), N_EXPERTS - 1).astype(jnp.int32)
    live = (blk0 < pend[-1]).astype(jnp.int32)
    fresh = jnp.concatenate([jnp.ones((1,), jnp.int32),
                             (block_expert[1:] != block_expert[:-1]).astype(jnp.int32)])
    slot = jnp.arange(N_SLOTS)
    se = jnp.repeat(block_expert, MOE_BLOCK)
    off = slot - pstart[se]
    valid = off < counts[se]
    src = jnp.clip(start[se] + off, 0, n * TOP_K - 1)
    flat = order[src]
    slot_tok = jnp.where(valid, flat // TOP_K, n - 1).astype(jnp.int32)
    slot_gate = jnp.where(valid, gate.reshape(-1)[flat], 0.0)
    slot_of = dest[jnp.argsort(order)].reshape(n, TOP_K)
    return slot_tok, slot_gate, block_expert, fresh, live, slot_of


def _final_kernel(npt, sa_ref, sb_ref, x_ref, y_hbm, gp, gs, lng, lnb, op_ref, os_ref, buf, sem):
    i = pl.program_id(0)
    nb = pl.num_programs(0)
    tm = x_ref.shape[0]

    def issue(blk, slot):
        def body(r, carry):
            t = blk * tm + r
            pltpu.make_async_copy(y_hbm.at[pl.ds(sa_ref[t], 1)], buf.at[slot, 0, pl.ds(r, 1)], sem.at[slot]).start()
            pltpu.make_async_copy(y_hbm.at[pl.ds(sb_ref[t], 1)], buf.at[slot, 1, pl.ds(r, 1)],
                                  sem.at[slot]).start(priority=1)
            return carry
        lax.fori_loop(0, tm, body, 0, unroll=8)

    @pl.when(i == 0)
    def _():
        issue(0, 0)

    @pl.when(i + 1 < nb)
    def _():
        issue(i + 1, (i + 1) % 2)

    slot = i % 2
    for half in range(TOP_K):
        pltpu.make_async_copy(y_hbm.at[pl.ds(0, tm)], buf.at[slot, half], sem.at[slot]).wait()
    y = buf[slot, 0] + buf[slot, 1]
    out = _layer_norm(ALPHA * x_ref[...] + _pick(i, npt, gp, gs) * y, lng[...], lnb[...])

    @pl.when(i < npt)
    def _():
        op_ref[...] = out

    @pl.when(i >= npt)
    def _():
        os_ref[...] = out


def _final(x, yb, slot_a, slot_b, mp, ms, lng, lnb, tm=256):
    npt = T_P // tm
    row = pl.BlockSpec((tm, D_MODEL), lambda i, *_: (i, 0))
    vec = pl.BlockSpec((1, D_MODEL), lambda i, *_: (0, 0))
    grid_spec = pltpu.PrefetchScalarGridSpec(
        num_scalar_prefetch=2,
        grid=(T_ALL // tm,),
        in_specs=[row, pl.BlockSpec(memory_space=pl.ANY), *_mod_specs(5, tm), vec, vec],
        out_specs=[pl.BlockSpec((tm, D_MODEL), lambda i, *_: (jnp.minimum(i, npt - 1), 0)),
                   pl.BlockSpec((tm, D_MODEL), lambda i, *_: (jnp.maximum(i - npt, 0), 0))],
        scratch_shapes=[pltpu.VMEM((2, 2, tm, D_MODEL), F32), pltpu.SemaphoreType.DMA((2,))],
    )
    return pl.pallas_call(
        functools.partial(_final_kernel, npt),
        grid_spec=grid_spec,
        out_shape=[jax.ShapeDtypeStruct((T_P, D_MODEL), F32), jax.ShapeDtypeStruct((T_S, D_MODEL), F32)],
        compiler_params=_params(("arbitrary",)),
        name="moe_combine_postnorm",
    )(slot_a, slot_b, x, yb, mp, ms, lng, lnb)


def kernel(x_prompt, x_sample, cache_cmp_kv, cache_sel_kv, state_win_kv, state_pool, page_table,
           c_prompt, c_sample, w_ada, b_ada, ln_gain, ln_bias, w_nsa_in, w_cmp1, w_cmp2, cmp_pos,
           w_nsa_out, w_pool, pool_scale, w_ffn_up, w_ffn_down, w_router, b_router, w_moe_up, w_moe_down):
    xp = x_prompt.reshape(T_P, D_MODEL)
    xs = x_sample.reshape(T_S, D_MODEL)
    mods = _ada(jnp.concatenate([c_prompt, c_sample], axis=0), w_ada, b_ada)

    def layer_mods(i):
        return mods[i, :BATCH].reshape(BATCH, 1, 6 * D_MODEL), jnp.repeat(mods[i, BATCH:], DEC_SEQ, axis=0)

    mp, ms = layer_mods(0)
    lng = ln_gain[0].reshape(2, 1, D_MODEL)
    lnb = ln_bias[0].reshape(2, 1, D_MODEL)
    h = _modulate(xp, xs, mp, ms, 0, 1)
    w_in_bf = w_nsa_in[0].astype(BF16)
    w_gate_bf = jnp.pad(w_in_bf[:, N_Q_COLS + 3 * KV_COLS:], ((0, 0), (0, 128 - N_GATE_COLS)))
    q, (cmp_kv, sel_kv, win_kv), gates = _nsa_project(h, w_in_bf, w_gate_bf)

    w1 = w_cmp1[0].astype(BF16)
    wc = w1.reshape(2, 2, CMP_STRIDE, HEAD_DIM, HEAD_DIM).transpose(0, 2, 3, 1, 4).reshape(
        2, CMP_STRIDE * HEAD_DIM, 2 * HEAD_DIM)
    cpos = jnp.pad(cmp_pos[0].reshape(2, 1, CMP_BLOCK * HEAD_DIM), ((0, 0), (0, 7), (0, 0)))
    cw = (wc, w1, w_cmp2[0].astype(BF16), cpos)
    stored = lambda a: a.reshape(-1, HEAD_DIM)
    kvc_p = _compress_prompt(cmp_kv[0], cw)
    kvc_s = _compress_sample(stored(cache_cmp_kv), page_table, cw)

    o = _attn_prompt(q, sel_kv[2], win_kv[2], kvc_p, gates, _overlap_t(SEQ // SEL_BLOCK, 32))
    ns_s = -(-(PAST_LEN + DEC_SEQ) // SEL_BLOCK)
    o_s, new_win_rows = _attn_sample(q[T_P:].astype(F32), sel_kv[1], win_kv[1], stored(cache_sel_kv),
                                     stored(state_win_kv), page_table, kvc_s, gates, _overlap_t(ns_s, 40))
    o = lax.dynamic_update_slice(o, o_s.astype(BF16), (T_P, 0))
    x1, h1 = _outproj(o, w_nsa_out[0].astype(BF16), xp, xs, mp, ms, lng[0], lnb[0])
    act = _ffn_up(h1, w_ffn_up[0].astype(BF16))
    x2 = _ffn_down(act, w_ffn_down[0].astype(BF16), x1, mp, ms, lng[1], lnb[1])

    kv_p = lambda a: a.reshape(1, BATCH, SEQ, 2, N_KV_HEADS, HEAD_DIM)
    kv_s = lambda a: a.reshape(1, DEC_BATCH, -1, 2, N_KV_HEADS, HEAD_DIM)
    new_cmp_p, new_cmp_s = kv_p(cmp_kv[0]), kv_s(cmp_kv[1])
    new_sel_p, new_sel_s = kv_p(sel_kv[0]), kv_s(sel_kv[1])
    new_win_p = kv_p(win_kv[0])[:, :, SEQ - WINDOW:]
    new_win_s = kv_s(new_win_rows)

    mp, ms = layer_mods(1)
    lng = ln_gain[1].reshape(2, 1, D_MODEL)
    lnb = ln_bias[1].reshape(2, 1, D_MODEL)
    wr_pad = jnp.pad(w_router[0], ((0, 0), (0, 128 - N_EXPERTS)))
    wr_hi, wr_lo = _split_bf16(wr_pad)
    br = jnp.pad(b_router[0].reshape(1, N_EXPERTS), ((0, 0), (0, 128 - N_EXPERTS)))
    pool_args = (w_pool[0].astype(BF16), pool_scale[0].reshape(1, D_MODEL), lng[0], lnb[0], wr_hi, wr_lo, br)
    x3, h3, hmp, lg = _pool_prompt(x2, mp, *pool_args)

    ms3 = ms.reshape(DEC_BATCH, DEC_SEQ, 6 * D_MODEL)
    ms_ext = jnp.pad(ms3, ((0, 0), (POOL_EXT - DEC_SEQ, 0), (0, 0))).reshape(DEC_BATCH * POOL_EXT, 6 * D_MODEL)
    xs_ext = jnp.concatenate([jnp.zeros((DEC_BATCH, 1, D_MODEL), F32), state_pool[0],
                              x2[T_P:].reshape(DEC_BATCH, DEC_SEQ, D_MODEL)], axis=1)
    xs_ext = xs_ext.reshape(DEC_BATCH * POOL_EXT, D_MODEL)
    x1s, h1s, hms, lgs = _pool_sample(xs_ext, ms_ext, *pool_args)
    new_rows = lambda a: a.reshape(DEC_BATCH, POOL_EXT, -1)[:, POOL_EXT - DEC_SEQ:].reshape(T_S, -1)
    x3 = lax.dynamic_update_slice(x3, new_rows(x1s), (T_P, 0))
    h3 = lax.dynamic_update_slice(h3, new_rows(h1s), (T_P, 0))
    logits = lax.dynamic_update_slice(lg, new_rows(lgs), (T_P, 0))[:, :N_EXPERTS]
    new_pool_p = hmp.reshape(BATCH, -1, D_MODEL)[None, :, -POOL_BUF:]
    new_pool_s = hms.reshape(DEC_BATCH, POOL_EXT, D_MODEL)[None, :, POOL_EXT - POOL_BUF:]

    slot_tok, slot_gate, block_expert, fresh, live, slot_of = _route(logits)
    xb = _gather_rows_bf16(h3, slot_tok)
    act = _moe_up(xb, w_moe_up[0], block_expert, fresh, live)
    yb = _moe_down(act, w_moe_down[0], slot_gate.reshape(-1, 1), block_expert, fresh, live)
    y_prompt, y_sample = _final(x3, yb, slot_of[:, 0], slot_of[:, 1], mp, ms, lng[1], lnb[1])

    return (y_prompt.reshape(BATCH, SEQ, D_MODEL), y_sample.reshape(DEC_BATCH, DEC_SEQ, D_MODEL),
            new_cmp_p, new_cmp_s, new_sel_p, new_sel_s, new_win_p, new_win_s, new_pool_p, new_pool_s)
```
